```python
import jax, jax.numpy as jnp
from jax import lax
import numpy as np

D_MODEL = 1024
BATCH = 32
SEQ = 256
DEPTH = 4
DEC_BATCH = 4
DEC_SEQ = 2048
PAST_LEN = 256

GRID_W = 64
N_MIXERS = 3
D_FF = 4 * D_MODEL
D_RNN = D_MODEL
LRU_BLOCK = 256
N_LRU_BLOCKS = D_RNN // LRU_BLOCK
LRU_CONV_W = 4
LRU_C = 8.0
CONF_CONV_W = 31
NA_HEAD_DIM = 64
NA_HEADS = D_MODEL // NA_HEAD_DIM
NA_WIN_ROWS = 8
NA_WIN_COLS = 16
Q_BLOCK = 128
ATT_SCALE = NA_HEAD_DIM ** -0.5
EPS = 1e-6
N_LRU_LAYERS = (DEPTH + 2) // 3
N_CONV_LAYERS = (DEPTH + 1) // 3
N_NA_LAYERS = DEPTH // 3

kernel_name = 'hybrid_lru_conformer_natten_diffusion_step'

F32 = jnp.float32


def _rmsnorm(x, g=None):
    xf = x.astype(F32)
    y = xf * lax.rsqrt(jnp.mean(xf * xf, axis=-1, keepdims=True) + EPS)
    if g is not None:
        y = y * g.astype(F32)
    return y.astype(x.dtype)


def _layernorm(x, g, b):
    xf = x.astype(F32)
    mu = jnp.mean(xf, axis=-1, keepdims=True)
    xc = xf - mu
    var = jnp.mean(xc * xc, axis=-1, keepdims=True)
    return (xc * lax.rsqrt(var + EPS) * g.astype(F32) + b.astype(F32)).astype(x.dtype)


def _adaln(cond, w, b):
    m = jax.nn.silu(cond) @ w + b
    return [t[:, None, :] for t in jnp.split(m, 6, axis=-1)]


def _modulate(x, shift, scale):
    return _rmsnorm(x) * (1 + scale) + shift


def _dwconv_centred(x, w, b):
    k = w.shape[0]
    lo = (k - 1) // 2
    hi = k - 1 - lo
    y = lax.conv_general_dilated(x, w[:, None, :], window_strides=(1,), padding=[(lo, hi)],
                                 dimension_numbers=('NWC', 'WIO', 'NWC'),
                                 feature_group_count=x.shape[-1])
    return y + b


def _sq_relu_mlp(h, w1, w2):
    return jnp.square(jax.nn.relu(h @ w1)) @ w2


def _block_diag(x, w, b):
    bsz, t, c = x.shape
    xb = x.reshape(bsz, t, N_LRU_BLOCKS, LRU_BLOCK)
    y = jnp.einsum('btni,nij->btnj', xb, w.astype(F32)).reshape(bsz, t, c)
    return y + b.astype(F32)


def _linear_scan(a, bx, h0):
    bx = bx.at[:, 0].add(a[:, 0] * h0)

    def comb(l, r):
        return l[0] * r[0], r[0] * l[1] + r[1]

    _, h = lax.associative_scan(comb, (a, bx), axis=1)
    return h


def _lru_direction(xf, w_a, b_a, w_x, b_x, lam, h0):
    r = jax.nn.sigmoid(_block_diag(xf, w_a, b_a))
    ig = jax.nn.sigmoid(_block_diag(xf, w_x, b_x))
    log_a = -LRU_C * r * jax.nn.softplus(-lam.astype(F32))
    a = jnp.exp(log_a)
    bx = jnp.sqrt(-jnp.expm1(2.0 * log_a)) * (ig * xf)
    return _linear_scan(a, bx, h0.astype(F32))


def _lru_mixer(h, w_in, conv_w, conv_b, w_a, b_a, w_x, b_x, lam, w_out, h0):
    gate_br, rec = jnp.split(h @ w_in, 2, axis=-1)
    xf = _dwconv_centred(rec, conv_w, conv_b).astype(F32)
    hs_f = _lru_direction(xf, w_a[0], b_a[0], w_x[0], b_x[0], lam[0], h0[:, 0])
    hs_b = jnp.flip(_lru_direction(jnp.flip(xf, 1), w_a[1], b_a[1], w_x[1], b_x[1], lam[1], h0[:, 1]), 1)
    final = jnp.stack([hs_f[:, -1], hs_b[:, 0]], axis=1)
    y = (hs_f + hs_b).astype(h.dtype) * jax.nn.gelu(gate_br, approximate=True)
    return y @ w_out, final


def _conformer_conv(h, w_pw1, b_pw1, dw_w, dw_b, ln_g, ln_b, w_pw2, b_pw2):
    val, gate = jnp.split(h @ w_pw1 + b_pw1, 2, axis=-1)
    z = val * jax.nn.sigmoid(gate)
    z = _dwconv_centred(z, dw_w, dw_b)
    z = jax.nn.silu(_layernorm(z, ln_g, ln_b))
    return z @ w_pw2 + b_pw2


def _split_qkv(h, w_qkv):
    bsz, t, _ = h.shape
    qkv = (h @ w_qkv).reshape(bsz, t, 3, NA_HEADS, NA_HEAD_DIM)
    return qkv[:, :, 0], qkv[:, :, 1], qkv[:, :, 2]


def _na_context(h, w_qkv, w_o):
    bsz, L, d = h.shape
    q, k, v = _split_qkv(h, w_qkv)
    nblk = L // Q_BLOCK
    qb = jnp.moveaxis(q.reshape(bsz, nblk, Q_BLOCK, NA_HEADS, NA_HEAD_DIM), 1, 0)

    def one(q_blk):
        s = jnp.einsum('bqhd,bkhd->bhqk', q_blk, k).astype(F32) * ATT_SCALE
        p = jax.nn.softmax(s, axis=-1).astype(v.dtype)
        return jnp.einsum('bhqk,bkhd->bqhd', p, v)

    o = jnp.moveaxis(lax.map(one, qb), 0, 1).reshape(bsz, L, d)
    return o @ w_o, k, v


def _na_latent(h, w_qkv, w_o, rpb, k_ctx, v_ctx):
    bsz, t, d = h.shape
    rows = t // GRID_W
    kr = min(NA_WIN_ROWS, rows)
    kc = NA_WIN_COLS
    q, k, v = _split_qkv(h, w_qkv)
    grid = (bsz, rows, GRID_W, NA_HEADS, NA_HEAD_DIM)
    q, k, v = q.reshape(grid), k.reshape(grid), v.reshape(grid)
    r_start = jnp.clip(jnp.arange(rows) - kr // 2, 0, rows - kr)
    cols = jnp.arange(GRID_W)
    col_win = jnp.clip(cols - kc // 2, 0, GRID_W - kc)[:, None] + jnp.arange(kc)[None, :]
    dc = col_win - cols[:, None] + (NA_WIN_COLS - 1)
    rpb_f = rpb.astype(F32)
    n_loc = kr * kc

    def one_row(args):
        q_r, r = args
        rs = r_start[r]
        k_win = lax.dynamic_slice_in_dim(k, rs, kr, axis=1)[:, :, col_win]
        v_win = lax.dynamic_slice_in_dim(v, rs, kr, axis=1)[:, :, col_win]
        dr = rs + jnp.arange(kr) - r + (NA_WIN_ROWS - 1)
        bias = rpb_f[:, dr[None, :, None], dc[:, None, :]]
        s_loc = jnp.einsum('bqhd,brqchd->bhqrc', q_r, k_win).astype(F32) * ATT_SCALE + bias[None]
        s_ctx = jnp.einsum('bqhd,bkhd->bhqk', q_r, k_ctx).astype(F32) * ATT_SCALE
        s = jnp.concatenate([s_loc.reshape(bsz, NA_HEADS, GRID_W, n_loc), s_ctx], axis=-1)
        p = jax.nn.softmax(s, axis=-1).astype(v.dtype)
        p_loc = p[..., :n_loc].reshape(bsz, NA_HEADS, GRID_W, kr, kc)
        p_ctx = p[..., n_loc:]
        return (jnp.einsum('bhqrc,brqchd->bqhd', p_loc, v_win)
                + jnp.einsum('bhqk,bkhd->bqhd', p_ctx, v_ctx))

    o = lax.map(one_row, (jnp.moveaxis(q, 1, 0), jnp.arange(rows)))
    o = jnp.moveaxis(o, 0, 1).reshape(bsz, t, d)
    return o @ w_o


def _trunk(x, cond, is_ctx, lru_h0, ctx_k, ctx_v, P):
    lru_states, ks, vs = [], [], []
    for i in range(DEPTH):
        sh1, sc1, g1, sh2, sc2, g2 = _adaln(cond, P['w_mod'][i], P['b_mod'][i])
        h = _modulate(x, sh1, sc1)
        kind, j = i % N_MIXERS, i // N_MIXERS
        if kind == 0:
            h0 = jnp.zeros((x.shape[0], 2, D_RNN), F32) if is_ctx else lru_h0[:, j]
            y, h_last = _lru_mixer(h, P['lru_w_in'][j], P['lru_conv_w'][j], P['lru_conv_b'][j],
                                   P['lru_w_a'][j], P['lru_b_a'][j], P['lru_w_x'][j], P['lru_b_x'][j],
                                   P['lru_lambda'][j], P['lru_w_out'][j], h0)
            if is_ctx:
                lru_states.append(h_last.astype(x.dtype))
        elif kind == 1:
            y = _conformer_conv(h, P['conf_w_pw1'][j], P['conf_b_pw1'][j], P['conf_dw_w'][j],
                                P['conf_dw_b'][j], P['conf_ln_g'][j], P['conf_ln_b'][j],
                                P['conf_w_pw2'][j], P['conf_b_pw2'][j])
        else:
            if is_ctx:
                y, k, v = _na_context(h, P['na_w_qkv'][j], P['na_w_o'][j])
                ks.append(k)
                vs.append(v)
            else:
                y = _na_latent(h, P['na_w_qkv'][j], P['na_w_o'][j], P['na_rpb'][j],
                               ctx_k[:, j], ctx_v[:, j])
        x = x + g1 * y
        h = _modulate(x, sh2, sc2)
        x = x + g2 * _sq_relu_mlp(h, P['w_ff1'][i], P['w_ff2'][i])
    return _rmsnorm(x, P['final_g']), lru_states, ks, vs


def _stack_layers(lst, empty_shape, dtype):
    return jnp.stack(lst, axis=1) if lst else jnp.zeros(empty_shape, dtype)


def setup_inputs(seed: int = 0) -> dict:
    key = jax.random.key(seed)
    ks = iter(jax.random.split(key, 48))

    def nrm(shape, s):
        return jax.random.normal(next(ks), shape, F32) * s

    d = D_MODEL
    u = jax.random.uniform(next(ks), (N_LRU_LAYERS, 2, D_RNN), F32, 0.9, 0.999)
    a_base = u ** (1.0 / LRU_C)
    lru_lambda = jnp.log(a_base) - jnp.log1p(-a_base)
    return {
        'x_prompt': nrm((BATCH, SEQ, d), 1.0),
        'x_sample': nrm((DEC_BATCH, DEC_SEQ, d), 1.0),
        'state_lru': nrm((DEC_BATCH, N_LRU_LAYERS, 2, D_RNN), 0.5),
        'cache_k': nrm((DEC_BATCH, N_NA_LAYERS, PAST_LEN, NA_HEADS, NA_HEAD_DIM), 1.0),
        'cache_v': nrm((DEC_BATCH, N_NA_LAYERS, PAST_LEN, NA_HEADS, NA_HEAD_DIM), 1.0),
        'c': nrm((DEC_BATCH, d), 1.0),
        'c_ctx': nrm((d,), 1.0),
        'w_mod': nrm((DEPTH, d, 6 * d), 0.5 * d ** -0.5),
        'b_mod': nrm((DEPTH, 6 * d), 0.02),
        'w_ff1': nrm((DEPTH, d, D_FF), d ** -0.5),
        'w_ff2': nrm((DEPTH, D_FF, d), D_FF ** -0.5),
        'lru_w_in': nrm((N_LRU_LAYERS, d, 2 * D_RNN), d ** -0.5),
        'lru_conv_w': nrm((N_LRU_LAYERS, LRU_CONV_W, D_RNN), LRU_CONV_W ** -0.5),
        'lru_conv_b': nrm((N_LRU_LAYERS, D_RNN), 0.02),
        'lru_w_a': nrm((N_LRU_LAYERS, 2, N_LRU_BLOCKS, LRU_BLOCK, LRU_BLOCK), LRU_BLOCK ** -0.5),
        'lru_b_a': nrm((N_LRU_LAYERS, 2, D_RNN), 0.02),
        'lru_w_x': nrm((N_LRU_LAYERS, 2, N_LRU_BLOCKS, LRU_BLOCK, LRU_BLOCK), LRU_BLOCK ** -0.5),
        'lru_b_x': nrm((N_LRU_LAYERS, 2, D_RNN), 0.02),
        'lru_lambda': lru_lambda,
        'lru_w_out': nrm((N_LRU_LAYERS, D_RNN, d), D_RNN ** -0.5),
        'conf_w_pw1': nrm((N_CONV_LAYERS, d, 2 * d), d ** -0.5),
        'conf_b_pw1': nrm((N_CONV_LAYERS, 2 * d), 0.02),
        'conf_dw_w': nrm((N_CONV_LAYERS, CONF_CONV_W, d), CONF_CONV_W ** -0.5),
        'conf_dw_b': nrm((N_CONV_LAYERS, d), 0.02),
        'conf_ln_g': 1.0 + nrm((N_CONV_LAYERS, d), 0.02),
        'conf_ln_b': nrm((N_CONV_LAYERS, d), 0.02),
        'conf_w_pw2': nrm((N_CONV_LAYERS, d, d), d ** -0.5),
        'conf_b_pw2': nrm((N_CONV_LAYERS, d), 0.02),
        'na_w_qkv': nrm((N_NA_LAYERS, d, 3 * d), d ** -0.5),
        'na_w_o': nrm((N_NA_LAYERS, d, d), d ** -0.5),
        'na_rpb': nrm((N_NA_LAYERS, NA_HEADS, 2 * NA_WIN_ROWS - 1, 2 * NA_WIN_COLS - 1), 0.1),
        'final_g': 1.0 + nrm((d,), 0.02),
    }


def reference(x_prompt, x_sample, state_lru, cache_k, cache_v, c, c_ctx,
              w_mod, b_mod, w_ff1, w_ff2,
              lru_w_in, lru_conv_w, lru_conv_b, lru_w_a, lru_b_a, lru_w_x, lru_b_x, lru_lambda, lru_w_out,
              conf_w_pw1, conf_b_pw1, conf_dw_w, conf_dw_b, conf_ln_g, conf_ln_b, conf_w_pw2, conf_b_pw2,
              na_w_qkv, na_w_o, na_rpb, final_g):
    P = {
        'w_mod': w_mod, 'b_mod': b_mod, 'w_ff1': w_ff1, 'w_ff2': w_ff2,
        'lru_w_in': lru_w_in, 'lru_conv_w': lru_conv_w, 'lru_conv_b': lru_conv_b,
        'lru_w_a': lru_w_a, 'lru_b_a': lru_b_a, 'lru_w_x': lru_w_x, 'lru_b_x': lru_b_x,
        'lru_lambda': lru_lambda, 'lru_w_out': lru_w_out,
        'conf_w_pw1': conf_w_pw1, 'conf_b_pw1': conf_b_pw1, 'conf_dw_w': conf_dw_w,
        'conf_dw_b': conf_dw_b, 'conf_ln_g': conf_ln_g, 'conf_ln_b': conf_ln_b,
        'conf_w_pw2': conf_w_pw2, 'conf_b_pw2': conf_b_pw2,
        'na_w_qkv': na_w_qkv, 'na_w_o': na_w_o, 'na_rpb': na_rpb, 'final_g': final_g,
    }
    y_prompt, lru_list, k_list, v_list = _trunk(x_prompt, c_ctx[None, :], True, None, None, None, P)
    y_sample, _, _, _ = _trunk(x_sample, c, False, state_lru, cache_k, cache_v, P)
    bp, lp = x_prompt.shape[0], x_prompt.shape[1]
    new_state_lru = _stack_layers(lru_list, (bp, 0, 2, D_RNN), x_prompt.dtype)
    new_cache_k = _stack_layers(k_list, (bp, 0, lp, NA_HEADS, NA_HEAD_DIM), x_prompt.dtype)
    new_cache_v = _stack_layers(v_list, (bp, 0, lp, NA_HEADS, NA_HEAD_DIM), x_prompt.dtype)
    return (y_prompt, y_sample, new_state_lru, new_cache_k, new_cache_v)
```

```python
import functools

import jax
import jax.numpy as jnp
from jax import lax
from jax.experimental import pallas as pl
from jax.experimental.pallas import tpu as pltpu

F32 = jnp.float32
BF16 = jnp.bfloat16

D_MODEL = 1024
D_FF = 4 * D_MODEL
PS = 256
CB = 256
NB = D_MODEL // CB
SLOTS = 8
GROUP_ROWS = SLOTS * PS
GRID_W = 64
GRID_H = 32
NA_HEADS = 16
NA_HEAD_DIM = 64
NA_WIN_ROWS = 8
NA_WIN_COLS = 16
ATT_SCALE = NA_HEAD_DIM ** -0.5
LRU_C = 8.0
LRU_CONV_W = 4
CONF_CONV_W = 31
EPS = 1e-6
NEG_BIG = -1e30

LANES = 128
TM = 1024
FK = 512
MOD_TILE = 1536
Q_ROWS = 4
Q_BLK = Q_ROWS * GRID_W
WIN_ROWS_BLK = 12
WIN_KEYS = WIN_ROWS_BLK * GRID_W
VMEM_LIMIT = 56 * 1024 * 1024


def _cparams(sem):
    return pltpu.CompilerParams(dimension_semantics=sem, vmem_limit_bytes=VMEM_LIMIT)


def _dot(a, b):
    return jnp.dot(a, b, preferred_element_type=F32)


def _dot_t(a, b):
    return lax.dot_general(a, b, (((1,), (1,)), ((), ())), preferred_element_type=F32)


def _rms(x):
    return x * lax.rsqrt(jnp.mean(x * x, axis=-1, keepdims=True) + EPS)


def _mod_parts(mod_ref, mod_row):
    m = mod_ref[pl.ds(mod_row(pl.program_id(0)), 1), :]
    return [m[:, k * D_MODEL:(k + 1) * D_MODEL] for k in range(6)]


def _adaln_kernel(c_ref, w_ref, b_ref, o_ref):
    c = c_ref[...]
    s = (c * jax.nn.sigmoid(c)).astype(BF16)
    o_ref[...] = _dot(s, w_ref[...].astype(BF16)) + b_ref[...]


def _adaln(cond8, w_mod, b_mod):
    depth = w_mod.shape[0]
    n_out = w_mod.shape[2]
    return pl.pallas_call(
        _adaln_kernel,
        grid=(depth, n_out // MOD_TILE),
        in_specs=[
            pl.BlockSpec((8, D_MODEL), lambda l, n: (0, 0)),
            pl.BlockSpec((None, D_MODEL, MOD_TILE), lambda l, n: (l, 0, n)),
            pl.BlockSpec((None, 1, MOD_TILE), lambda l, n: (l, 0, n)),
        ],
        out_specs=pl.BlockSpec((None, 8, MOD_TILE), lambda l, n: (l, 0, n)),
        out_shape=jax.ShapeDtypeStruct((depth, 8, n_out), F32),
        compiler_params=_cparams(("parallel", "parallel")),
        name="adaln",
    )(cond8, w_mod, b_mod.reshape(depth, 1, n_out))


def _modmm_kernel(*refs, kind, mod_row):
    if kind == "lru":
        x_ref, mod_ref, wg_ref, wr_ref, og_ref, or_ref, h_ref = refs
    elif kind == "glu":
        x_ref, mod_ref, wv_ref, wgt_ref, bv_ref, bgt_ref, oz_ref, h_ref = refs
    else:
        x_ref, mod_ref, wq_ref, wk_ref, wv_ref, oq_ref, ok_ref, ov_ref, h_ref = refs

    @pl.when(pl.program_id(1) == 0)
    def _():
        sh1, sc1 = _mod_parts(mod_ref, mod_row)[:2]
        h_ref[...] = (_rms(x_ref[...]) * (1.0 + sc1) + sh1).astype(BF16)

    h = h_ref[...]
    if kind == "lru":
        og_ref[...] = jax.nn.gelu(_dot(h, wg_ref[...].astype(BF16)), approximate=True)
        or_ref[...] = _dot(h, wr_ref[...].astype(BF16))
    elif kind == "glu":
        val = _dot(h, wv_ref[...].astype(BF16)) + bv_ref[...]
        gate = _dot(h, wgt_ref[...].astype(BF16)) + bgt_ref[...]
        oz_ref[...] = val * jax.nn.sigmoid(gate)
    else:
        oq_ref[...] = _dot(h, wq_ref[...].astype(BF16)).astype(oq_ref.dtype)
        ok_ref[...] = _dot(h, wk_ref[...].astype(BF16)).astype(ok_ref.dtype)
        ov_ref[...] = _dot(h, wv_ref[...].astype(BF16)).astype(ov_ref.dtype)


def _modmm(x2, mods_l, mod_row, kind, w, b=None, kv_dtype=BF16):
    rows = x2.shape[0]
    n_groups_w = w.shape[1] // D_MODEL

    def col_spec(shape, g):
        return pl.BlockSpec(shape, lambda i, n: (0, g * NB + n))

    in_specs = [pl.BlockSpec((TM, D_MODEL), lambda i, n: (i, 0)),
                pl.BlockSpec((8, 6 * D_MODEL), lambda i, n: (0, 0))]
    in_specs += [col_spec((D_MODEL, CB), g) for g in range(n_groups_w)]
    args = [x2, mods_l] + [w] * n_groups_w
    nat = pl.BlockSpec((TM, CB), lambda i, n: (i, n))
    if kind == "lru":
        out_dtypes = [F32, F32]
    elif kind == "glu":
        b2 = b.reshape(1, -1)
        args += [b2, b2]
        in_specs += [col_spec((1, CB), g) for g in range(2)]
        out_dtypes = [F32]
    else:
        out_dtypes = [BF16, kv_dtype, kv_dtype]
    return pl.pallas_call(
        functools.partial(_modmm_kernel, kind=kind, mod_row=mod_row),
        grid=(rows // TM, NB),
        in_specs=in_specs,
        out_specs=[nat] * len(out_dtypes),
        out_shape=[jax.ShapeDtypeStruct((rows, D_MODEL), dt) for dt in out_dtypes],
        scratch_shapes=[pltpu.VMEM((TM, D_MODEL), BF16)],
        compiler_params=_cparams(("parallel", "arbitrary")),
        name="modmm_" + kind,
    )(*args)


def _to_time_major(x2):
    return jnp.swapaxes(x2.reshape(SLOTS, PS, CB), 0, 1)


def _from_time_major(x3):
    return jnp.swapaxes(x3, 0, 1).reshape(GROUP_ROWS, CB)


def _slot_iota():
    return lax.broadcasted_iota(jnp.int32, (SLOTS, CB), 0)


def _from_prev_slot(tile):
    return jnp.where(_slot_iota() == 0, 0.0, pltpu.roll(tile, 1, 0))


def _from_next_slot(tile):
    return jnp.where(_slot_iota() == SLOTS - 1, 0.0, pltpu.roll(tile, SLOTS - 1, 0))


def _fill_padded(pad_ref, x_ref, lo, hi, chunked):
    pad_ref[lo:lo + PS] = _to_time_major(x_ref[...])
    for r in range(lo):
        if chunked:
            pad_ref[r] = _from_prev_slot(pad_ref[PS + r])
        else:
            pad_ref[r] = jnp.zeros((SLOTS, CB), F32)
    for r in range(hi):
        if chunked:
            pad_ref[lo + PS + r] = _from_next_slot(pad_ref[lo + r])
        else:
            pad_ref[lo + PS + r] = jnp.zeros((SLOTS, CB), F32)


def _group_spec():
    return pl.BlockSpec((GROUP_ROWS, CB), lambda n, g: (g, n))


LRU_TC = 32


def _softplus(x):
    return jnp.maximum(x, 0.0) + jnp.log1p(jnp.exp(-jnp.abs(x)))


def _lru_seq_kernel(*refs, chunked):
    if chunked:
        (rec_ref, gate_ref, cw_ref, cb_ref, wa_ref, ba_ref, wx_ref, bx_ref, lam_ref, h0_ref,
         y_ref, pad_ref, af_ref, bf_ref, ab_ref, bb_ref) = refs
    else:
        (rec_ref, gate_ref, cw_ref, cb_ref, wa_ref, ba_ref, wx_ref, bx_ref, lam_ref,
         y_ref, st_ref, pad_ref, af_ref, bf_ref, ab_ref, bb_ref) = refs
    lo = (LRU_CONV_W - 1) // 2
    hi = LRU_CONV_W - 1 - lo
    _fill_padded(pad_ref, rec_ref, lo, hi, chunked)

    a_refs = (af_ref, ab_ref)
    b_refs = (bf_ref, bb_ref)
    neg_c_sp = [-LRU_C * _softplus(-lam_ref[d:d + 1, :]) for d in range(2)]

    def gates(ci, carry):
        t0 = pl.multiple_of(ci * LRU_TC, LRU_TC)
        xf = cb_ref[...] + cw_ref[0:1, :] * pad_ref[pl.ds(t0, LRU_TC)]
        for k in range(1, LRU_CONV_W):
            xf = xf + cw_ref[k:k + 1, :] * pad_ref[pl.ds(t0 + k, LRU_TC)]
        x2 = xf.reshape(LRU_TC * SLOTS, CB)
        xb = x2.astype(BF16)
        for d in range(2):
            r = jax.nn.sigmoid(_dot(xb, wa_ref[d].astype(BF16)) + ba_ref[d:d + 1, :])
            ig = jax.nn.sigmoid(_dot(xb, wx_ref[d].astype(BF16)) + bx_ref[d:d + 1, :])
            log_a = neg_c_sp[d] * r
            a = jnp.exp(log_a)
            bx = jnp.sqrt(-jnp.tanh(log_a) * (a * a + 1.0)) * (ig * x2)
            a_refs[d][pl.ds(t0, LRU_TC)] = a.reshape(LRU_TC, SLOTS, CB)
            b_refs[d][pl.ds(t0, LRU_TC)] = bx.reshape(LRU_TC, SLOTS, CB)
        return carry

    lax.fori_loop(0, PS // LRU_TC, gates, 0)

    zero = jnp.zeros((SLOTS, CB), F32)
    one = jnp.ones((SLOTS, CB), F32)

    def scan(t, carry):
        hf, hb, pf, pb = carry
        tb = PS - 1 - t
        a = af_ref[t]
        hf = a * hf + bf_ref[t]
        bf_ref[t] = hf
        ar = ab_ref[tb]
        hb = ar * hb + bb_ref[tb]
        bb_ref[tb] = hb
        if chunked:
            pf = a * pf
            af_ref[t] = pf
            pb = ar * pb
            ab_ref[tb] = pb
        return hf, hb, pf, pb

    lax.fori_loop(0, PS, scan, (zero, zero, one, one), unroll=4)

    if chunked:
        slot = _slot_iota()
        h0f = jnp.broadcast_to(h0_ref[0:1, :], (SLOTS, CB))
        h0b = jnp.broadcast_to(h0_ref[1:2, :], (SLOTS, CB))
        end_f, prod_f = bf_ref[PS - 1], af_ref[PS - 1]
        end_b, prod_b = bb_ref[0], ab_ref[0]
        in_f = jnp.where(slot == 0, h0f, 0.0)
        in_b = jnp.where(slot == SLOTS - 1, h0b, 0.0)
        for _ in range(SLOTS - 1):
            in_f = jnp.where(slot == 0, h0f, pltpu.roll(end_f + prod_f * in_f, 1, 0))
            in_b = jnp.where(slot == SLOTS - 1, h0b,
                             pltpu.roll(end_b + prod_b * in_b, SLOTS - 1, 0))
    else:
        st_ref[0] = bf_ref[PS - 1]
        st_ref[1] = bb_ref[0]

    def combine(ci, carry):
        sl = pl.ds(pl.multiple_of(ci * LRU_TC, LRU_TC), LRU_TC)
        hs = bf_ref[sl] + bb_ref[sl]
        if chunked:
            hs = hs + af_ref[sl] * in_f + ab_ref[sl] * in_b
        bf_ref[sl] = hs
        return carry

    lax.fori_loop(0, PS // LRU_TC, combine, 0)
    y_ref[...] = (_from_time_major(bf_ref[...]) * gate_ref[...]).astype(y_ref.dtype)


def _lru_seq(rec, gate, p, j, h0=None):
    chunked = h0 is not None
    rows = rec.shape[0]
    groups = rows // GROUP_ROWS
    cw, cb = p["lru_conv_w"][j], p["lru_conv_b"][j].reshape(1, D_MODEL)
    wa, ba, wx, bx = p["lru_w_a"][j], p["lru_b_a"][j], p["lru_w_x"][j], p["lru_b_x"][j]
    lam = p["lru_lambda"][j]
    seq = _group_spec()
    vec2 = pl.BlockSpec((2, CB), lambda n, g: (0, n))
    wblk = pl.BlockSpec((2, None, CB, CB), lambda n, g: (0, n, 0, 0))
    in_specs = [seq, seq,
                pl.BlockSpec((LRU_CONV_W, CB), lambda n, g: (0, n)),
                pl.BlockSpec((1, CB), lambda n, g: (0, n)),
                wblk, vec2, wblk, vec2, vec2]
    args = [rec, gate, cw, cb, wa, ba, wx, bx, lam]
    y_shape = jax.ShapeDtypeStruct((rows, D_MODEL), BF16)
    if chunked:
        in_specs.append(pl.BlockSpec((None, 2, CB), lambda n, g: (g, 0, n)))
        args.append(h0)
        out_specs = [seq]
        out_shape = [y_shape]
    else:
        out_specs = [seq, pl.BlockSpec((2, SLOTS, CB), lambda n, g: (0, g, n))]
        out_shape = [y_shape, jax.ShapeDtypeStruct((2, groups * SLOTS, D_MODEL), F32)]
    tile = (PS, SLOTS, CB)
    outs = pl.pallas_call(
        functools.partial(_lru_seq_kernel, chunked=chunked),
        grid=(NB, groups),
        in_specs=in_specs,
        out_specs=out_specs,
        out_shape=out_shape,
        scratch_shapes=[pltpu.VMEM((PS + LRU_CONV_W - 1, SLOTS, CB), F32)]
        + [pltpu.VMEM(tile, F32) for _ in range(4)],
        compiler_params=_cparams(("parallel", "parallel")),
        name="lru_seq_dec" if chunked else "lru_seq_ctx",
    )(*args)
    return (outs[0], None) if chunked else (outs[0], outs[1])


CONV_TC = 16


def _conv_seq_kernel(z_ref, w_ref, b_ref, o_ref, pad_ref, out_ref, *, chunked):
    lo = (CONF_CONV_W - 1) // 2
    hi = CONF_CONV_W - 1 - lo
    _fill_padded(pad_ref, z_ref, lo, hi, chunked)

    def chunk(ci, carry):
        t0 = pl.multiple_of(ci * CONV_TC, CONV_TC)
        acc = b_ref[...] + w_ref[0:1, :] * pad_ref[pl.ds(t0, CONV_TC)]
        for k in range(1, CONF_CONV_W):
            acc = acc + w_ref[k:k + 1, :] * pad_ref[pl.ds(t0 + k, CONV_TC)]
        out_ref[pl.ds(t0, CONV_TC)] = acc
        return carry

    lax.fori_loop(0, PS // CONV_TC, chunk, 0)
    o_ref[...] = _from_time_major(out_ref[...])


def _conv_seq(z, w, b, chunked):
    rows = z.shape[0]
    seq = _group_spec()
    return pl.pallas_call(
        functools.partial(_conv_seq_kernel, chunked=chunked),
        grid=(NB, rows // GROUP_ROWS),
        in_specs=[seq,
                  pl.BlockSpec((CONF_CONV_W, CB), lambda n, g: (0, n)),
                  pl.BlockSpec((1, CB), lambda n, g: (0, n))],
        out_specs=seq,
        out_shape=jax.ShapeDtypeStruct((rows, D_MODEL), F32),
        scratch_shapes=[pltpu.VMEM((PS + CONF_CONV_W - 1, SLOTS, CB), F32),
                        pltpu.VMEM((PS, SLOTS, CB), F32)],
        compiler_params=_cparams(("parallel", "parallel")),
        name="conv_seq_dec" if chunked else "conv_seq_ctx",
    )(z, w, b.reshape(1, D_MODEL))


def _attend(q, parts):
    lane = lax.broadcasted_iota(jnp.int32, (1, LANES), 1)
    outs = []
    for par in range(2):
        sel = (lane < NA_HEAD_DIM) if par == 0 else (lane >= NA_HEAD_DIM)
        qm = jnp.where(sel, q, jnp.zeros_like(q))
        scores = []
        for k, _, bias in parts:
            s = _dot_t(qm, k) * ATT_SCALE
            if bias is not None:
                s = s + bias[par]
            scores.append(s)
        mx = scores[0].max(axis=-1, keepdims=True)
        for s in scores[1:]:
            mx = jnp.maximum(mx, s.max(axis=-1, keepdims=True))
        den = 0.0
        acc = 0.0
        for s, (_, v, _) in zip(scores, parts):
            pr = jnp.exp(s - mx)
            den = den + pr.sum(axis=-1, keepdims=True)
            acc = acc + _dot(pr.astype(BF16), v)
        outs.append(acc / den)
    return jnp.where(lane < NA_HEAD_DIM, outs[0], outs[1])


def _attn_ctx_kernel(q_ref, k_ref, v_ref, o_ref):
    k = k_ref[...].astype(BF16)
    v = v_ref[...].astype(BF16)
    o_ref[...] = _attend(q_ref[...], [(k, v, None)]).astype(o_ref.dtype)


def _attn_ctx(q, k, v, batch):
    blk = pl.BlockSpec((PS, LANES), lambda b, j: (b, j))
    return pl.pallas_call(
        _attn_ctx_kernel,
        grid=(batch, D_MODEL // LANES),
        in_specs=[blk, blk, blk],
        out_specs=blk,
        out_shape=jax.ShapeDtypeStruct(q.shape, BF16),
        compiler_params=_cparams(("parallel", "parallel")),
        name="attn_ctx",
    )(q, k, v)


def _attn_dec_kernel(q_ref, k_ref, v_ref, kc_ref, vc_ref, bias_ref, o_ref):
    i = pl.program_id(2)
    row0 = jnp.clip(Q_ROWS * i - NA_WIN_ROWS // 2, 0, GRID_H - WIN_ROWS_BLK)
    start = pl.multiple_of(row0 * GRID_W, GRID_W)
    kw = k_ref[pl.ds(start, WIN_KEYS), :]
    vw = v_ref[pl.ds(start, WIN_KEYS), :]
    kc = kc_ref[...].astype(BF16)
    vc = vc_ref[...].astype(BF16)
    o = _attend(q_ref[...], [(kw, vw, bias_ref), (kc, vc, None)])
    o_ref[...] = o.astype(o_ref.dtype)


def _attn_bias_table(rpb):
    tables = []
    for i in (0, 1, GRID_H // Q_ROWS - 1):
        row0 = min(max(Q_ROWS * i - NA_WIN_ROWS // 2, 0), GRID_H - WIN_ROWS_BLK)
        r = Q_ROWS * i + jnp.arange(Q_ROWS)
        rk = row0 + jnp.arange(WIN_ROWS_BLK)
        rs = jnp.clip(r - NA_WIN_ROWS // 2, 0, GRID_H - NA_WIN_ROWS)
        ok_r = (rk[None, :] >= rs[:, None]) & (rk[None, :] < rs[:, None] + NA_WIN_ROWS)
        dr = jnp.clip(rk[None, :] - r[:, None] + NA_WIN_ROWS - 1, 0, 2 * NA_WIN_ROWS - 2)
        c = jnp.arange(GRID_W)
        cs = jnp.clip(c - NA_WIN_COLS // 2, 0, GRID_W - NA_WIN_COLS)
        ok_c = (c[None, :] >= cs[:, None]) & (c[None, :] < cs[:, None] + NA_WIN_COLS)
        dc = jnp.clip(c[None, :] - c[:, None] + NA_WIN_COLS - 1, 0, 2 * NA_WIN_COLS - 2)
        vals = rpb[:, dr[:, None, :, None], dc[None, :, None, :]]
        ok = ok_r[:, None, :, None] & ok_c[None, :, None, :]
        tables.append(jnp.where(ok[None], vals, NEG_BIG).reshape(NA_HEADS, Q_BLK, WIN_KEYS))
    return jnp.stack(tables)


def _attn_dec(q, k, v, kc, vc, bias, batch):
    t = GRID_H * GRID_W
    n_i = GRID_H // Q_ROWS
    qblk = pl.BlockSpec((Q_BLK, LANES), lambda b, j, i: (b * n_i + i, j))
    kvblk = pl.BlockSpec((None, t, LANES), lambda b, j, i: (b, 0, j))
    cblk = pl.BlockSpec((None, PS, LANES), lambda b, j, i: (b, 0, j))

    def bias_idx(b, j, i):
        cls = (i > 0).astype(jnp.int32) + (i == n_i - 1).astype(jnp.int32)
        return (cls, j, 0, 0)

    return pl.pallas_call(
        _attn_dec_kernel,
        grid=(batch, D_MODEL // LANES, n_i),
        in_specs=[qblk, kvblk, kvblk, cblk, cblk,
                  pl.BlockSpec((None, 2, Q_BLK, WIN_KEYS), bias_idx)],
        out_specs=qblk,
        out_shape=jax.ShapeDtypeStruct(q.shape, BF16),
        compiler_params=_cparams(("parallel", "parallel", "arbitrary")),
        name="attn_dec",
    )(q, k.reshape(batch, t, D_MODEL), v.reshape(batch, t, D_MODEL), kc, vc, bias)


def _pffn_kernel(*refs, conf, final, mod_row):
    refs = list(refs)
    x_ref, y_ref, mod_ref, wp_ref = refs[:4]
    pos = 4
    bp_ref = lng_ref = lnb_ref = fin_ref = None
    if conf:
        bp_ref, lng_ref, lnb_ref = refs[pos:pos + 3]
        pos += 3
    w1_ref, w2_ref = refs[pos:pos + 2]
    pos += 2
    if final:
        fin_ref = refs[pos]
        pos += 1
    o_ref, h2_ref, acc_ref = refs[pos:pos + 3]
    f = pl.program_id(1)

    @pl.when(f == 0)
    def _():
        _, _, g1, sh2, sc2, _ = _mod_parts(mod_ref, mod_row)
        if conf:
            z = y_ref[...]
            mu = jnp.mean(z, axis=-1, keepdims=True)
            zc = z - mu
            var = jnp.mean(zc * zc, axis=-1, keepdims=True)
            zn = zc * lax.rsqrt(var + EPS) * lng_ref[...] + lnb_ref[...]
            y = (zn * jax.nn.sigmoid(zn)).astype(BF16)
        else:
            y = y_ref[...]
        proj = _dot(y, wp_ref[...].astype(BF16))
        if bp_ref is not None:
            proj = proj + bp_ref[...]
        x1 = x_ref[...] + g1 * proj
        o_ref[...] = x1
        h2_ref[...] = (_rms(x1) * (1.0 + sc2) + sh2).astype(BF16)
        acc_ref[...] = jnp.zeros_like(acc_ref)

    u = _dot(h2_ref[...], w1_ref[...].astype(BF16))
    u = jnp.square(jnp.maximum(u, 0.0)).astype(BF16)
    acc_ref[...] += _dot(u, w2_ref[...].astype(BF16))

    @pl.when(f == pl.num_programs(1) - 1)
    def _():
        g2 = _mod_parts(mod_ref, mod_row)[5]
        out = o_ref[...] + g2 * acc_ref[...]
        if final:
            out = _rms(out) * fin_ref[...]
        o_ref[...] = out


def _pffn(x2, y, mods_l, mod_row, w_proj, w1, w2, b_proj=None, ln_g=None, ln_b=None,
          final_g=None):
    rows = x2.shape[0]
    conf = ln_g is not None
    final = final_g is not None
    row_spec = pl.BlockSpec((TM, D_MODEL), lambda i, f: (i, 0))
    vec_spec = pl.BlockSpec((1, D_MODEL), lambda i, f: (0, 0))
    in_specs = [row_spec, row_spec,
                pl.BlockSpec((8, 6 * D_MODEL), lambda i, f: (0, 0)),
                pl.BlockSpec((D_MODEL, D_MODEL), lambda i, f: (0, 0),
                             pipeline_mode=pl.Buffered(1))]
    args = [x2, y, mods_l, w_proj]
    if conf:
        in_specs += [vec_spec, vec_spec, vec_spec]
        args += [b_proj.reshape(1, -1), ln_g.reshape(1, -1), ln_b.reshape(1, -1)]
    in_specs += [pl.BlockSpec((D_MODEL, FK), lambda i, f: (0, f)),
                 pl.BlockSpec((FK, D_MODEL), lambda i, f: (f, 0))]
    args += [w1, w2]
    if final:
        in_specs.append(vec_spec)
        args.append(final_g.reshape(1, -1))
    return pl.pallas_call(
        functools.partial(_pffn_kernel, conf=conf, final=final, mod_row=mod_row),
        grid=(rows // TM, D_FF // FK),
        in_specs=in_specs,
        out_specs=row_spec,
        out_shape=jax.ShapeDtypeStruct((rows, D_MODEL), F32),
        scratch_shapes=[pltpu.VMEM((TM, D_MODEL), BF16), pltpu.VMEM((TM, D_MODEL), F32)],
        compiler_params=_cparams(("parallel", "arbitrary")),
        name="pffn" + ("_conf" if conf else "") + ("_final" if final else ""),
    )(*args)


def _trunk(x, mods, is_ctx, p, state_lru, cache_k, cache_v, bias_tab):
    bsz, t, d = x.shape
    rows = bsz * t
    x2 = x.reshape(rows, d)
    depth = mods.shape[0]
    seq_per_tile = max(TM // t, 1)
    tiles_per_seq = max(t // TM, 1)
    if is_ctx:
        mod_row = lambda i: 0
    else:
        mod_row = lambda i: 1 + (i * seq_per_tile) // tiles_per_seq
    states, ks, vs = [], [], []
    for i in range(depth):
        kind, j = i % 3, i // 3
        m = mods[i]
        fin = p["final_g"] if i == depth - 1 else None
        if kind == 0:
            gate, rec = _modmm(x2, m, mod_row, "lru", p["lru_w_in"][j])
            h0 = None if is_ctx else state_lru[:, j]
            y, st = _lru_seq(rec, gate, p, j, h0)
            if is_ctx:
                states.append(st)
            x2 = _pffn(x2, y, m, mod_row, p["lru_w_out"][j], p["w_ff1"][i], p["w_ff2"][i],
                       final_g=fin)
        elif kind == 1:
            (z,) = _modmm(x2, m, mod_row, "glu", p["conf_w_pw1"][j], p["conf_b_pw1"][j])
            zc = _conv_seq(z, p["conf_dw_w"][j], p["conf_dw_b"][j], chunked=not is_ctx)
            x2 = _pffn(x2, zc, m, mod_row, p["conf_w_pw2"][j], p["w_ff1"][i], p["w_ff2"][i],
                       b_proj=p["conf_b_pw2"][j], ln_g=p["conf_ln_g"][j], ln_b=p["conf_ln_b"][j],
                       final_g=fin)
        else:
            if is_ctx:
                q, k, v = _modmm(x2, m, mod_row, "qkv", p["na_w_qkv"][j], kv_dtype=F32)
                o = _attn_ctx(q, k, v, bsz)
                ks.append(k.reshape(bsz, t, NA_HEADS, NA_HEAD_DIM))
                vs.append(v.reshape(bsz, t, NA_HEADS, NA_HEAD_DIM))
            else:
                q, k, v = _modmm(x2, m, mod_row, "qkv", p["na_w_qkv"][j], kv_dtype=BF16)
                kc = cache_k[:, j].reshape(bsz, -1, d)
                vc = cache_v[:, j].reshape(bsz, -1, d)
                o = _attn_dec(q, k, v, kc, vc, bias_tab[j], bsz)
            x2 = _pffn(x2, o, m, mod_row, p["na_w_o"][j], p["w_ff1"][i], p["w_ff2"][i],
                       final_g=fin)
    return x2.reshape(bsz, t, d), states, ks, vs


def kernel(x_prompt, x_sample, state_lru, cache_k, cache_v, c, c_ctx, w_mod, b_mod, w_ff1, w_ff2, lru_w_in, lru_conv_w, lru_conv_b, lru_w_a, lru_b_a, lru_w_x, lru_b_x, lru_lambda, lru_w_out, conf_w_pw1, conf_b_pw1, conf_dw_w, conf_dw_b, conf_ln_g, conf_ln_b, conf_w_pw2, conf_b_pw2, na_w_qkv, na_w_o, na_rpb, final_g):
    p = dict(w_ff1=w_ff1, w_ff2=w_ff2, lru_w_in=lru_w_in, lru_conv_w=lru_conv_w,
             lru_conv_b=lru_conv_b, lru_w_a=lru_w_a, lru_b_a=lru_b_a, lru_w_x=lru_w_x,
             lru_b_x=lru_b_x, lru_lambda=lru_lambda, lru_w_out=lru_w_out,
             conf_w_pw1=conf_w_pw1, conf_b_pw1=conf_b_pw1, conf_dw_w=conf_dw_w,
             conf_dw_b=conf_dw_b, conf_ln_g=conf_ln_g, conf_ln_b=conf_ln_b,
             conf_w_pw2=conf_w_pw2, conf_b_pw2=conf_b_pw2, na_w_qkv=na_w_qkv, na_w_o=na_w_o,
             final_g=final_g)
    dec_b = c.shape[0]
    assert 1 + dec_b <= 8
    cond8 = jnp.concatenate([c_ctx[None, :], c, jnp.zeros((8 - 1 - dec_b, D_MODEL), F32)], axis=0)
    mods = _adaln(cond8, w_mod, b_mod)
    bias_tab = [_attn_bias_table(na_rpb[j]) for j in range(na_rpb.shape[0])]

    y_prompt, states, ks, vs = _trunk(x_prompt, mods, True, p, None, None, None, None)
    y_sample, _, _, _ = _trunk(x_sample, mods, False, p, state_lru, cache_k, cache_v, bias_tab)

    new_state = jnp.stack([jnp.transpose(s, (1, 0, 2)) for s in states], axis=1)
    new_k = jnp.stack(ks, axis=1)
    new_v = jnp.stack(vs, axis=1)
    return (y_prompt, y_sample, new_state, new_k, new_v)
```

```python
import functools

import jax
import jax.numpy as jnp
from jax import lax
from jax.experimental import pallas as pl
from jax.experimental.pallas import tpu as pltpu

F32 = jnp.float32
BF16 = jnp.bfloat16

D_MODEL = 1024
D_FF = 4 * D_MODEL
PS = 256
CB = 256
NB = D_MODEL // CB
SLOTS = 8
GROUP_ROWS = SLOTS * PS
GRID_W = 64
GRID_H = 32
NA_HEADS = 16
NA_HEAD_DIM = 64
NA_WIN_ROWS = 8
NA_WIN_COLS = 16
ATT_SCALE = NA_HEAD_DIM ** -0.5
LRU_C = 8.0
LRU_CONV_W = 4
CONF_CONV_W = 31
EPS = 1e-6
NEG_BIG = -1e30

LANES = 128
TM = 1024
TM_MM = 512
FK = 512
MOD_TILE = 1536
Q_ROWS = 4
Q_BLK = Q_ROWS * GRID_W
N_QBLK = GRID_H // Q_ROWS
WIN_ROWS_BLK = 12
WIN_KEYS = WIN_ROWS_BLK * GRID_W
VMEM_LIMIT = 56 * 1024 * 1024


def _cparams(sem):
    return pltpu.CompilerParams(dimension_semantics=sem, vmem_limit_bytes=VMEM_LIMIT)


def _dot(a, b):
    return jnp.dot(a, b, preferred_element_type=F32)


def _dot_t(a, b):
    return lax.dot_general(a, b, (((1,), (1,)), ((), ())), preferred_element_type=F32)


def _rms(x):
    return x * lax.rsqrt(jnp.mean(x * x, axis=-1, keepdims=True) + EPS)


def _sigmoid(x):
    return 0.5 * jnp.tanh(0.5 * x) + 0.5


def _mod_parts(mod_ref, mod_row):
    m = mod_ref[pl.ds(mod_row(pl.program_id(0)), 1), :]
    return [m[:, k * D_MODEL:(k + 1) * D_MODEL] for k in range(6)]


def _make_mod_row(is_ctx, seq_len, tile_rows):
    if is_ctx:
        return lambda i: 0
    return lambda i: 1 + (i * tile_rows) // seq_len


def _adaln_kernel(c_ref, w_ref, b_ref, o_ref):
    c = c_ref[...]
    s = (c * jax.nn.sigmoid(c)).astype(BF16)
    o_ref[...] = _dot(s, w_ref[...].astype(BF16)) + b_ref[...]


def _adaln(cond8, w_mod, b_mod):
    depth = w_mod.shape[0]
    n_out = w_mod.shape[2]
    return pl.pallas_call(
        _adaln_kernel,
        grid=(depth, n_out // MOD_TILE),
        in_specs=[
            pl.BlockSpec((8, D_MODEL), lambda l, n: (0, 0)),
            pl.BlockSpec((None, D_MODEL, MOD_TILE), lambda l, n: (l, 0, n)),
            pl.BlockSpec((None, 1, MOD_TILE), lambda l, n: (l, 0, n)),
        ],
        out_specs=pl.BlockSpec((None, 8, MOD_TILE), lambda l, n: (l, 0, n)),
        out_shape=jax.ShapeDtypeStruct((depth, 8, n_out), F32),
        compiler_params=_cparams(("parallel", "parallel")),
        name="adaln",
    )(cond8, w_mod, b_mod.reshape(depth, 1, n_out))


def _modmm_kernel(*refs, kind, mod_row):
    if kind == "glu":
        x_ref, mod_ref, w_ref, b_ref = refs[:4]
        outs = refs[4:-1]
    else:
        x_ref, mod_ref, w_ref = refs[:3]
        outs = refs[3:-1]
    wbf_ref = refs[-1]

    @pl.when(pl.program_id(0) == 0)
    def _():
        wbf_ref[...] = w_ref[...].astype(BF16)

    sh1, sc1 = _mod_parts(mod_ref, mod_row)[:2]
    h = (_rms(x_ref[...]) * (1.0 + sc1) + sh1).astype(BF16)

    def mm(g):
        return _dot(h, wbf_ref[:, g * D_MODEL:(g + 1) * D_MODEL])

    if kind == "lru":
        outs[0][...] = jax.nn.gelu(mm(0), approximate=True)
        outs[1][...] = mm(1)
    elif kind == "glu":
        val = mm(0) + b_ref[:, :D_MODEL]
        gate = mm(1) + b_ref[:, D_MODEL:]
        outs[0][...] = val * _sigmoid(gate)
    else:
        for g in range(3):
            outs[g][...] = mm(g).astype(outs[g].dtype)


def _modmm(x2, mods, layer, mod_row, kind, w, j, b=None, kv_dtype=BF16):
    rows = x2.shape[0]
    n_cols = w.shape[2]
    row_spec = pl.BlockSpec((TM_MM, D_MODEL), lambda i: (i, 0))
    in_specs = [row_spec,
                pl.BlockSpec((None, 8, 6 * D_MODEL), lambda i: (layer, 0, 0)),
                pl.BlockSpec((None, D_MODEL, n_cols), lambda i: (j, 0, 0),
                             pipeline_mode=pl.Buffered(1))]
    args = [x2, mods, w]
    if kind == "lru":
        out_dtypes = [F32, F32]
    elif kind == "glu":
        in_specs.append(pl.BlockSpec((None, 1, n_cols), lambda i: (j, 0, 0)))
        args.append(b.reshape(b.shape[0], 1, n_cols))
        out_dtypes = [F32]
    else:
        out_dtypes = [BF16, kv_dtype, kv_dtype]
    return pl.pallas_call(
        functools.partial(_modmm_kernel, kind=kind, mod_row=mod_row),
        grid=(rows // TM_MM,),
        in_specs=in_specs,
        out_specs=[row_spec] * len(out_dtypes),
        out_shape=[jax.ShapeDtypeStruct((rows, D_MODEL), dt) for dt in out_dtypes],
        scratch_shapes=[pltpu.VMEM((D_MODEL, n_cols), BF16)],
        compiler_params=_cparams(("arbitrary",)),
        name="modmm_" + kind,
    )(*args)


def _to_time_major(x2):
    return jnp.swapaxes(x2.reshape(SLOTS, PS, CB), 0, 1)


def _from_time_major(x3):
    return jnp.swapaxes(x3, 0, 1).reshape(GROUP_ROWS, CB)


def _slot_iota():
    return lax.broadcasted_iota(jnp.int32, (SLOTS, CB), 0)


def _from_prev_slot(tile):
    return jnp.where(_slot_iota() == 0, 0.0, pltpu.roll(tile, 1, 0))


def _from_next_slot(tile):
    return jnp.where(_slot_iota() == SLOTS - 1, 0.0, pltpu.roll(tile, SLOTS - 1, 0))


def _fill_padded(pad_ref, x_ref, lo, hi, chunked):
    pad_ref[lo:lo + PS] = _to_time_major(x_ref[...])
    for r in range(lo):
        if chunked:
            pad_ref[r] = _from_prev_slot(pad_ref[PS + r])
        else:
            pad_ref[r] = jnp.zeros((SLOTS, CB), F32)
    for r in range(hi):
        if chunked:
            pad_ref[lo + PS + r] = _from_next_slot(pad_ref[lo + r])
        else:
            pad_ref[lo + PS + r] = jnp.zeros((SLOTS, CB), F32)


def _group_spec():
    return pl.BlockSpec((GROUP_ROWS, CB), lambda n, g: (g, n))


def _chan_spec(lead, j):
    return pl.BlockSpec((None, lead, CB), lambda n, g: (j, 0, n))


LRU_TC = 32


def _softplus(x):
    return jnp.maximum(x, 0.0) + jnp.log1p(jnp.exp(-jnp.abs(x)))


def _lru_seq_kernel(*refs, chunked):
    if chunked:
        (rec_ref, gate_ref, cw_ref, cb_ref, wa_ref, ba_ref, wx_ref, bx_ref, lam_ref, h0_ref,
         y_ref, pad_ref, af_ref, bf_ref, ab_ref, bb_ref) = refs
    else:
        (rec_ref, gate_ref, cw_ref, cb_ref, wa_ref, ba_ref, wx_ref, bx_ref, lam_ref,
         y_ref, st_ref, pad_ref, af_ref, bf_ref, ab_ref, bb_ref) = refs
    lo = (LRU_CONV_W - 1) // 2
    hi = LRU_CONV_W - 1 - lo
    _fill_padded(pad_ref, rec_ref, lo, hi, chunked)

    a_refs = (af_ref, ab_ref)
    b_refs = (bf_ref, bb_ref)
    neg_c_sp = [-LRU_C * _softplus(-lam_ref[d:d + 1, :]) for d in range(2)]

    def gates(ci, carry):
        t0 = pl.multiple_of(ci * LRU_TC, LRU_TC)
        xf = cb_ref[...] + cw_ref[0:1, :] * pad_ref[pl.ds(t0, LRU_TC)]
        for k in range(1, LRU_CONV_W):
            xf = xf + cw_ref[k:k + 1, :] * pad_ref[pl.ds(t0 + k, LRU_TC)]
        x2 = xf.reshape(LRU_TC * SLOTS, CB)
        xb = x2.astype(BF16)
        for d in range(2):
            r = _sigmoid(_dot(xb, wa_ref[d].astype(BF16)) + ba_ref[d:d + 1, :])
            ig = _sigmoid(_dot(xb, wx_ref[d].astype(BF16)) + bx_ref[d:d + 1, :])
            log_a = neg_c_sp[d] * r
            a = jnp.exp(log_a)
            one_m_a2 = -jnp.tanh(log_a) * (a * a + 1.0)
            root = jnp.where(one_m_a2 > 0.0, one_m_a2 * lax.rsqrt(one_m_a2), 0.0)
            bx = root * (ig * x2)
            a_refs[d][pl.ds(t0, LRU_TC)] = a.reshape(LRU_TC, SLOTS, CB)
            b_refs[d][pl.ds(t0, LRU_TC)] = bx.reshape(LRU_TC, SLOTS, CB)
        return carry

    lax.fori_loop(0, PS // LRU_TC, gates, 0)

    zero = jnp.zeros((SLOTS, CB), F32)
    one = jnp.ones((SLOTS, CB), F32)

    def scan(t, carry):
        hf, hb, pf, pb = carry
        tb = PS - 1 - t
        a = af_ref[t]
        hf = a * hf + bf_ref[t]
        bf_ref[t] = hf
        ar = ab_ref[tb]
        hb = ar * hb + bb_ref[tb]
        bb_ref[tb] = hb
        if chunked:
            pf = a * pf
            af_ref[t] = pf
            pb = ar * pb
            ab_ref[tb] = pb
        return hf, hb, pf, pb

    lax.fori_loop(0, PS, scan, (zero, zero, one, one), unroll=4)

    if chunked:
        slot = _slot_iota()
        h0f = jnp.broadcast_to(h0_ref[0:1, :], (SLOTS, CB))
        h0b = jnp.broadcast_to(h0_ref[1:2, :], (SLOTS, CB))
        end_f, prod_f = bf_ref[PS - 1], af_ref[PS - 1]
        end_b, prod_b = bb_ref[0], ab_ref[0]
        in_f = jnp.where(slot == 0, h0f, 0.0)
        in_b = jnp.where(slot == SLOTS - 1, h0b, 0.0)
        for _ in range(SLOTS - 1):
            in_f = jnp.where(slot == 0, h0f, pltpu.roll(end_f + prod_f * in_f, 1, 0))
            in_b = jnp.where(slot == SLOTS - 1, h0b,
                             pltpu.roll(end_b + prod_b * in_b, SLOTS - 1, 0))
    else:
        st_ref[0] = bf_ref[PS - 1]
        st_ref[1] = bb_ref[0]

    def combine(ci, carry):
        sl = pl.ds(pl.multiple_of(ci * LRU_TC, LRU_TC), LRU_TC)
        hs = bf_ref[sl] + bb_ref[sl]
        if chunked:
            hs = hs + af_ref[sl] * in_f + ab_ref[sl] * in_b
        bf_ref[sl] = hs
        return carry

    lax.fori_loop(0, PS // LRU_TC, combine, 0)
    y_ref[...] = (_from_time_major(bf_ref[...]) * gate_ref[...]).astype(y_ref.dtype)


def _lru_seq(rec, gate, p, j, state_lru=None):
    chunked = state_lru is not None
    rows = rec.shape[0]
    groups = rows // GROUP_ROWS
    n_layers = p["lru_conv_b"].shape[0]
    seq = _group_spec()
    wblk = pl.BlockSpec((None, 2, None, CB, CB), lambda n, g: (j, 0, n, 0, 0))
    in_specs = [seq, seq, _chan_spec(LRU_CONV_W, j), _chan_spec(1, j),
                wblk, _chan_spec(2, j), wblk, _chan_spec(2, j), _chan_spec(2, j)]
    args = [rec, gate, p["lru_conv_w"], p["lru_conv_b"].reshape(n_layers, 1, D_MODEL),
            p["lru_w_a"], p["lru_b_a"], p["lru_w_x"], p["lru_b_x"], p["lru_lambda"]]
    y_shape = jax.ShapeDtypeStruct((rows, D_MODEL), BF16)
    if chunked:
        in_specs.append(pl.BlockSpec((None, None, 2, CB), lambda n, g: (g, j, 0, n)))
        args.append(state_lru)
        out_specs = [seq]
        out_shape = [y_shape]
    else:
        out_specs = [seq, pl.BlockSpec((2, SLOTS, CB), lambda n, g: (0, g, n))]
        out_shape = [y_shape, jax.ShapeDtypeStruct((2, groups * SLOTS, D_MODEL), F32)]
    tile = (PS, SLOTS, CB)
    outs = pl.pallas_call(
        functools.partial(_lru_seq_kernel, chunked=chunked),
        grid=(NB, groups),
        in_specs=in_specs,
        out_specs=out_specs,
        out_shape=out_shape,
        scratch_shapes=[pltpu.VMEM((PS + LRU_CONV_W - 1, SLOTS, CB), F32)]
        + [pltpu.VMEM(tile, F32) for _ in range(4)],
        compiler_params=_cparams(("parallel", "parallel")),
        name="lru_seq_dec" if chunked else "lru_seq_ctx",
    )(*args)
    return (outs[0], None) if chunked else (outs[0], outs[1])


CONV_TC = 16


def _conv_seq_kernel(z_ref, w_ref, b_ref, o_ref, pad_ref, out_ref, *, chunked):
    lo = (CONF_CONV_W - 1) // 2
    hi = CONF_CONV_W - 1 - lo
    _fill_padded(pad_ref, z_ref, lo, hi, chunked)

    def chunk(ci, carry):
        t0 = pl.multiple_of(ci * CONV_TC, CONV_TC)
        acc = b_ref[...] + w_ref[0:1, :] * pad_ref[pl.ds(t0, CONV_TC)]
        for k in range(1, CONF_CONV_W):
            acc = acc + w_ref[k:k + 1, :] * pad_ref[pl.ds(t0 + k, CONV_TC)]
        out_ref[pl.ds(t0, CONV_TC)] = acc
        return carry

    lax.fori_loop(0, PS // CONV_TC, chunk, 0)
    o_ref[...] = _from_time_major(out_ref[...])


def _conv_seq(z, w, b, j, chunked):
    rows = z.shape[0]
    seq = _group_spec()
    return pl.pallas_call(
        functools.partial(_conv_seq_kernel, chunked=chunked),
        grid=(NB, rows // GROUP_ROWS),
        in_specs=[seq, _chan_spec(CONF_CONV_W, j), _chan_spec(1, j)],
        out_specs=seq,
        out_shape=jax.ShapeDtypeStruct((rows, D_MODEL), F32),
        scratch_shapes=[pltpu.VMEM((PS + CONF_CONV_W - 1, SLOTS, CB), F32),
                        pltpu.VMEM((PS, SLOTS, CB), F32)],
        compiler_params=_cparams(("parallel", "parallel")),
        name="conv_seq_dec" if chunked else "conv_seq_ctx",
    )(z, w, b.reshape(b.shape[0], 1, D_MODEL))


def _attend(q, parts):
    lane = lax.broadcasted_iota(jnp.int32, (1, LANES), 1)
    outs = []
    for par in range(2):
        sel = (lane < NA_HEAD_DIM) if par == 0 else (lane >= NA_HEAD_DIM)
        qm = jnp.where(sel, q, jnp.zeros_like(q))
        scores = []
        for k, _, bias in parts:
            s = _dot_t(qm, k) * ATT_SCALE
            if bias is not None:
                s = s + bias[par]
            scores.append(s)
        mx = scores[0].max(axis=-1, keepdims=True)
        for s in scores[1:]:
            mx = jnp.maximum(mx, s.max(axis=-1, keepdims=True))
        den = 0.0
        acc = 0.0
        for s, (_, v, _) in zip(scores, parts):
            pr = jnp.exp(s - mx)
            den = den + pr.sum(axis=-1, keepdims=True)
            acc = acc + _dot(pr.astype(BF16), v)
        outs.append(acc / den)
    return jnp.where(lane < NA_HEAD_DIM, outs[0], outs[1])


def _attn_ctx_kernel(q_ref, k_ref, v_ref, o_ref):
    for s in range(D_MODEL // LANES):
        sl = slice(s * LANES, (s + 1) * LANES)
        k = k_ref[:, sl].astype(BF16)
        v = v_ref[:, sl].astype(BF16)
        o_ref[:, sl] = _attend(q_ref[:, sl], [(k, v, None)]).astype(o_ref.dtype)


def _attn_ctx(q, k, v, batch):
    blk = pl.BlockSpec((PS, D_MODEL), lambda b: (b, 0))
    return pl.pallas_call(
        _attn_ctx_kernel,
        grid=(batch,),
        in_specs=[blk, blk, blk],
        out_specs=blk,
        out_shape=jax.ShapeDtypeStruct(q.shape, BF16),
        compiler_params=_cparams(("parallel",)),
        name="attn_ctx",
    )(q, k, v)


def _attn_dec_kernel(q_ref, k_ref, v_ref, kc_ref, vc_ref, bias_ref, o_ref):
    i = pl.program_id(2)
    row0 = jnp.clip(Q_ROWS * i - NA_WIN_ROWS // 2, 0, GRID_H - WIN_ROWS_BLK)
    start = pl.multiple_of(row0 * GRID_W, GRID_W)
    kw = k_ref[pl.ds(start, WIN_KEYS), :]
    vw = v_ref[pl.ds(start, WIN_KEYS), :]
    kc = kc_ref[...].astype(BF16)
    vc = vc_ref[...].astype(BF16)
    o = _attend(q_ref[...], [(kw, vw, bias_ref), (kc, vc, None)])
    o_ref[...] = o.astype(o_ref.dtype)


def _qblk_window(i):
    row0 = min(max(Q_ROWS * i - NA_WIN_ROWS // 2, 0), GRID_H - WIN_ROWS_BLK)
    out = []
    for a in range(Q_ROWS):
        r = Q_ROWS * i + a
        rs = min(max(r - NA_WIN_ROWS // 2, 0), GRID_H - NA_WIN_ROWS)
        out.append((r, [rs <= row0 + w < rs + NA_WIN_ROWS for w in range(WIN_ROWS_BLK)]))
    return row0, out


BIAS_CLASSES = (0, 1, N_QBLK - 1)


def _bias_kernel(rpb_ref, o_ref):
    c = lax.broadcasted_iota(jnp.int32, (GRID_W, LANES), 0)
    l = lax.broadcasted_iota(jnp.int32, (GRID_W, LANES), 1)
    cs = jnp.clip(c - NA_WIN_COLS // 2, 0, GRID_W - NA_WIN_COLS)
    in_cols = (l >= cs) & (l < cs + NA_WIN_COLS)
    neg = jnp.full((GRID_W, LANES), NEG_BIG, F32)
    lo_half, hi_half = [], []
    for dr in range(2 * NA_WIN_ROWS - 1):
        row = jnp.broadcast_to(rpb_ref[dr:dr + 1, :], (GRID_W, LANES))
        t = pltpu.roll(row, LANES - (NA_WIN_COLS - 1), 1, stride=1, stride_axis=0)
        t = jnp.where(in_cols, t, NEG_BIG)
        lo_half.append(t)
        hi_half.append(pltpu.roll(t, GRID_W, 1))
    for cls, i in enumerate(BIAS_CLASSES):
        row0, qrows = _qblk_window(i)
        for a, (r, valid) in enumerate(qrows):
            for wp in range(WIN_ROWS_BLK // 2):
                halves = []
                for half, bank in enumerate((lo_half, hi_half)):
                    w = 2 * wp + half
                    halves.append(bank[row0 + w - r + NA_WIN_ROWS - 1] if valid[w] else neg)
                o_ref[cls, a * GRID_W:(a + 1) * GRID_W, wp * LANES:(wp + 1) * LANES] = (
                    jnp.where(l < GRID_W, halves[0], halves[1]))


def _attn_bias_table(rpb, j):
    nl, nh, ndr, ndc = rpb.shape
    rpb_p = jnp.pad(rpb, ((0, 0), (0, 0), (0, 16 - ndr), (0, LANES - ndc)))
    return pl.pallas_call(
        _bias_kernel,
        grid=(nh,),
        in_specs=[pl.BlockSpec((None, None, 16, LANES), lambda h: (j, h, 0, 0))],
        out_specs=pl.BlockSpec((len(BIAS_CLASSES), None, Q_BLK, WIN_KEYS), lambda h: (0, h, 0, 0)),
        out_shape=jax.ShapeDtypeStruct((len(BIAS_CLASSES), nh, Q_BLK, WIN_KEYS), F32),
        compiler_params=_cparams(("parallel",)),
        name="attn_bias",
    )(rpb_p)


def _attn_dec(q, k, v, kc, vc, j, bias, batch):
    t = GRID_H * GRID_W
    qblk = pl.BlockSpec((Q_BLK, LANES), lambda b, s, i: (b * N_QBLK + i, s))
    kvblk = pl.BlockSpec((None, t, LANES), lambda b, s, i: (b, 0, s))
    cblk = pl.BlockSpec((None, None, PS, LANES), lambda b, s, i: (b, j, 0, s))

    def bias_idx(b, s, i):
        cls = (i > 0).astype(jnp.int32) + (i == N_QBLK - 1).astype(jnp.int32)
        return (cls, s, 0, 0)

    return pl.pallas_call(
        _attn_dec_kernel,
        grid=(batch, D_MODEL // LANES, N_QBLK),
        in_specs=[qblk, kvblk, kvblk, cblk, cblk,
                  pl.BlockSpec((None, 2, Q_BLK, WIN_KEYS), bias_idx)],
        out_specs=qblk,
        out_shape=jax.ShapeDtypeStruct(q.shape, BF16),
        compiler_params=_cparams(("parallel", "parallel", "arbitrary")),
        name="attn_dec",
    )(q, k.reshape(batch, t, D_MODEL), v.reshape(batch, t, D_MODEL), kc, vc, bias)


def _pffn_kernel(*refs, conf, final, mod_row):
    refs = list(refs)
    x_ref, y_ref, mod_ref, wp_ref = refs[:4]
    pos = 4
    bp_ref = lng_ref = lnb_ref = fin_ref = None
    if conf:
        bp_ref, lng_ref, lnb_ref = refs[pos:pos + 3]
        pos += 3
    w1_ref, w2_ref = refs[pos:pos + 2]
    pos += 2
    if final:
        fin_ref = refs[pos]
        pos += 1
    o_ref, h2_ref, acc_ref = refs[pos:pos + 3]
    f = pl.program_id(1)

    @pl.when(f == 0)
    def _():
        _, _, g1, sh2, sc2, _ = _mod_parts(mod_ref, mod_row)
        if conf:
            z = y_ref[...]
            mu = jnp.mean(z, axis=-1, keepdims=True)
            zc = z - mu
            var = jnp.mean(zc * zc, axis=-1, keepdims=True)
            zn = zc * lax.rsqrt(var + EPS) * lng_ref[...] + lnb_ref[...]
            y = (zn * _sigmoid(zn)).astype(BF16)
        else:
            y = y_ref[...]
        proj = _dot(y, wp_ref[...].astype(BF16))
        if bp_ref is not None:
            proj = proj + bp_ref[...]
        x1 = x_ref[...] + g1 * proj
        o_ref[...] = x1
        h2_ref[...] = (_rms(x1) * (1.0 + sc2) + sh2).astype(BF16)
        acc_ref[...] = jnp.zeros_like(acc_ref)

    u = _dot(h2_ref[...], w1_ref[...].astype(BF16))
    u = jnp.square(jnp.maximum(u, 0.0)).astype(BF16)
    acc_ref[...] += _dot(u, w2_ref[...].astype(BF16))

    @pl.when(f == pl.num_programs(1) - 1)
    def _():
        g2 = _mod_parts(mod_ref, mod_row)[5]
        out = o_ref[...] + g2 * acc_ref[...]
        if final:
            out = _rms(out) * fin_ref[...]
        o_ref[...] = out


def _pffn(x2, y, mods, layer, mod_row, w_proj, j, w1, w2, b_proj=None, ln_g=None, ln_b=None,
          final_g=None):
    rows = x2.shape[0]
    conf = ln_g is not None
    final = final_g is not None
    row_spec = pl.BlockSpec((TM, D_MODEL), lambda i, f: (i, 0))
    vec_spec = pl.BlockSpec((None, 1, D_MODEL), lambda i, f: (j, 0, 0))
    in_specs = [row_spec, row_spec,
                pl.BlockSpec((None, 8, 6 * D_MODEL), lambda i, f: (layer, 0, 0)),
                pl.BlockSpec((None, D_MODEL, D_MODEL), lambda i, f: (j, 0, 0),
                             pipeline_mode=pl.Buffered(1))]
    args = [x2, y, mods, w_proj]
    if conf:
        in_specs += [vec_spec, vec_spec, vec_spec]
        args += [v.reshape(v.shape[0], 1, D_MODEL) for v in (b_proj, ln_g, ln_b)]
    in_specs += [pl.BlockSpec((None, D_MODEL, FK), lambda i, f: (layer, 0, f)),
                 pl.BlockSpec((None, FK, D_MODEL), lambda i, f: (layer, f, 0))]
    args += [w1, w2]
    if final:
        in_specs.append(pl.BlockSpec((1, D_MODEL), lambda i, f: (0, 0)))
        args.append(final_g.reshape(1, D_MODEL))
    return pl.pallas_call(
        functools.partial(_pffn_kernel, conf=conf, final=final, mod_row=mod_row),
        grid=(rows // TM, D_FF // FK),
        in_specs=in_specs,
        out_specs=row_spec,
        out_shape=jax.ShapeDtypeStruct((rows, D_MODEL), F32),
        scratch_shapes=[pltpu.VMEM((TM, D_MODEL), BF16), pltpu.VMEM((TM, D_MODEL), F32)],
        compiler_params=_cparams(("parallel", "arbitrary")),
        name="pffn" + ("_conf" if conf else "") + ("_final" if final else ""),
    )(*args)


def _trunk(x, mods, is_ctx, p, state_lru, cache_k, cache_v, bias_tabs):
    bsz, t, d = x.shape
    rows = bsz * t
    x2 = x.reshape(rows, d)
    depth = mods.shape[0]
    row_mm = _make_mod_row(is_ctx, t, TM_MM)
    row_ffn = _make_mod_row(is_ctx, t, TM)
    states, ks, vs = [], [], []
    for i in range(depth):
        kind, j = i % 3, i // 3
        fin = p["final_g"] if i == depth - 1 else None
        ffn = functools.partial(_pffn, x2, mods=mods, layer=i, mod_row=row_ffn, j=j,
                                w1=p["w_ff1"], w2=p["w_ff2"], final_g=fin)
        if kind == 0:
            gate, rec = _modmm(x2, mods, i, row_mm, "lru", p["lru_w_in"], j)
            y, st = _lru_seq(rec, gate, p, j, None if is_ctx else state_lru)
            if is_ctx:
                states.append(st)
            x2 = ffn(y=y, w_proj=p["lru_w_out"])
        elif kind == 1:
            (z,) = _modmm(x2, mods, i, row_mm, "glu", p["conf_w_pw1"], j, b=p["conf_b_pw1"])
            zc = _conv_seq(z, p["conf_dw_w"], p["conf_dw_b"], j, chunked=not is_ctx)
            x2 = ffn(y=zc, w_proj=p["conf_w_pw2"], b_proj=p["conf_b_pw2"],
                     ln_g=p["conf_ln_g"], ln_b=p["conf_ln_b"])
        else:
            if is_ctx:
                q, k, v = _modmm(x2, mods, i, row_mm, "qkv", p["na_w_qkv"], j, kv_dtype=F32)
                o = _attn_ctx(q, k, v, bsz)
                ks.append(k.reshape(bsz, t, NA_HEADS, NA_HEAD_DIM))
                vs.append(v.reshape(bsz, t, NA_HEADS, NA_HEAD_DIM))
            else:
                q, k, v = _modmm(x2, mods, i, row_mm, "qkv", p["na_w_qkv"], j, kv_dtype=BF16)
                kc = cache_k.reshape(cache_k.shape[:3] + (d,))
                vc = cache_v.reshape(cache_v.shape[:3] + (d,))
                o = _attn_dec(q, k, v, kc, vc, j, bias_tabs[j], bsz)
            x2 = ffn(y=o, w_proj=p["na_w_o"])
    return x2.reshape(bsz, t, d), states, ks, vs


def kernel(x_prompt, x_sample, state_lru, cache_k, cache_v, c, c_ctx, w_mod, b_mod, w_ff1, w_ff2, lru_w_in, lru_conv_w, lru_conv_b, lru_w_a, lru_b_a, lru_w_x, lru_b_x, lru_lambda, lru_w_out, conf_w_pw1, conf_b_pw1, conf_dw_w, conf_dw_b, conf_ln_g, conf_ln_b, conf_w_pw2, conf_b_pw2, na_w_qkv, na_w_o, na_rpb, final_g):
    p = dict(w_ff1=w_ff1, w_ff2=w_ff2, lru_w_in=lru_w_in, lru_conv_w=lru_conv_w,
             lru_conv_b=lru_conv_b, lru_w_a=lru_w_a, lru_b_a=lru_b_a, lru_w_x=lru_w_x,
             lru_b_x=lru_b_x, lru_lambda=lru_lambda, lru_w_out=lru_w_out,
             conf_w_pw1=conf_w_pw1, conf_b_pw1=conf_b_pw1, conf_dw_w=conf_dw_w,
             conf_dw_b=conf_dw_b, conf_ln_g=conf_ln_g, conf_ln_b=conf_ln_b,
             conf_w_pw2=conf_w_pw2, conf_b_pw2=conf_b_pw2, na_w_qkv=na_w_qkv, na_w_o=na_w_o,
             final_g=final_g)
    dec_b = c.shape[0]
    assert 1 + dec_b <= 8
    cond8 = jnp.concatenate([c_ctx[None, :], c, jnp.zeros((8 - 1 - dec_b, D_MODEL), F32)], axis=0)
    mods = _adaln(cond8, w_mod, b_mod)
    bias_tabs = [_attn_bias_table(na_rpb, j) for j in range(na_rpb.shape[0])]

    y_prompt, states, ks, vs = _trunk(x_prompt, mods, True, p, None, None, None, None)
    y_sample, _, _, _ = _trunk(x_sample, mods, False, p, state_lru, cache_k, cache_v, bias_tabs)

    new_state = jnp.stack([jnp.transpose(s, (1, 0, 2)) for s in states], axis=1)
    new_k = jnp.stack(ks, axis=1)
    new_v = jnp.stack(vs, axis=1)
    return (y_prompt, y_sample, new_state, new_k, new_v)
```

```python
import functools
import math

import jax
import jax.numpy as jnp
from jax import lax
from jax.experimental import pallas as pl
from jax.experimental.pallas import tpu as pltpu

F32 = jnp.float32
BF16 = jnp.bfloat16

D_MODEL = 1024
D_FF = 4 * D_MODEL
PS = 256
CB = 256
NB = D_MODEL // CB
SLOTS = 8
GROUP_ROWS = SLOTS * PS
GRID_W = 64
GRID_H = 32
NA_HEADS = 16
NA_HEAD_DIM = 64
NA_WIN_ROWS = 8
NA_WIN_COLS = 16
ATT_SCALE = NA_HEAD_DIM ** -0.5
assert math.frexp(ATT_SCALE)[0] == 0.5, "the attention kernels scale bf16 queries exactly"
LRU_C = 8.0
LRU_CONV_W = 4
CONF_CONV_W = 31
EPS = 1e-6
NEG_BIG = -1e30

LANES = 128
TM = 1024
TM_MM = 512
FK = 1024
CAST_TILE = 1024
MOD_TILE = 1536
Q_ROWS = 4
Q_BLK = Q_ROWS * GRID_W
N_QBLK = GRID_H // Q_ROWS
WIN_ROWS_BLK = 12
WIN_KEYS = WIN_ROWS_BLK * GRID_W
DEC_SLABS = 4
VMEM_LIMIT = 56 * 1024 * 1024


def _cparams(sem):
    return pltpu.CompilerParams(dimension_semantics=sem, vmem_limit_bytes=VMEM_LIMIT)


def _dot(a, b):
    return jnp.dot(a, b, preferred_element_type=F32)


def _dot_t(a, b):
    return lax.dot_general(a, b, (((1,), (1,)), ((), ())), preferred_element_type=F32)


def _rms(x):
    return x * lax.rsqrt(jnp.mean(x * x, axis=-1, keepdims=True) + EPS)


def _sigmoid(x):
    return 0.5 * jnp.tanh(0.5 * x) + 0.5


def _mod_parts(mod_ref, mod_row):
    m = mod_ref[pl.ds(mod_row(pl.program_id(0)), 1), :]
    return [m[:, k * D_MODEL:(k + 1) * D_MODEL] for k in range(6)]


def _make_mod_row(is_ctx, seq_len, tile_rows):
    if is_ctx:
        return lambda i: 0
    return lambda i: 1 + (i * tile_rows) // seq_len


def _adaln_kernel(c_ref, w_ref, b_ref, o_ref):
    c = c_ref[...]
    s = (c * jax.nn.sigmoid(c)).astype(BF16)
    o_ref[...] = _dot(s, w_ref[...].astype(BF16)) + b_ref[...]


def _adaln(cond8, w_mod, b_mod):
    depth = w_mod.shape[0]
    n_out = w_mod.shape[2]
    return pl.pallas_call(
        _adaln_kernel,
        grid=(depth, n_out // MOD_TILE),
        in_specs=[
            pl.BlockSpec((8, D_MODEL), lambda l, n: (0, 0)),
            pl.BlockSpec((None, D_MODEL, MOD_TILE), lambda l, n: (l, 0, n)),
            pl.BlockSpec((None, 1, MOD_TILE), lambda l, n: (l, 0, n)),
        ],
        out_specs=pl.BlockSpec((None, 8, MOD_TILE), lambda l, n: (l, 0, n)),
        out_shape=jax.ShapeDtypeStruct((depth, 8, n_out), F32),
        compiler_params=_cparams(("parallel", "parallel")),
        name="adaln",
    )(cond8, w_mod, b_mod.reshape(depth, 1, n_out))


def _cast_kernel(w_ref, o_ref):
    o_ref[...] = w_ref[...].astype(o_ref.dtype)


def _to_bf16(w):
    nl, a, b = w.shape
    blk = pl.BlockSpec((None, CAST_TILE, CAST_TILE), lambda l, i, k: (l, i, k))
    return pl.pallas_call(
        _cast_kernel,
        grid=(nl, a // CAST_TILE, b // CAST_TILE),
        in_specs=[blk],
        out_specs=blk,
        out_shape=jax.ShapeDtypeStruct(w.shape, BF16),
        compiler_params=_cparams(("parallel", "parallel", "parallel")),
        name="cast_bf16",
    )(w)


def _modmm_kernel(*refs, kind, mod_row):
    if kind == "glu":
        x_ref, mod_ref, w_ref, b_ref = refs[:4]
        outs = refs[4:-1]
    else:
        x_ref, mod_ref, w_ref = refs[:3]
        outs = refs[3:-1]
    wbf_ref = refs[-1]

    @pl.when(pl.program_id(0) == 0)
    def _():
        wbf_ref[...] = w_ref[...].astype(BF16)

    sh1, sc1 = _mod_parts(mod_ref, mod_row)[:2]
    h = (_rms(x_ref[...]) * (1.0 + sc1) + sh1).astype(BF16)

    def mm(g):
        return _dot(h, wbf_ref[:, g * D_MODEL:(g + 1) * D_MODEL])

    if kind == "lru":
        outs[0][...] = jax.nn.gelu(mm(0), approximate=True)
        outs[1][...] = mm(1)
    elif kind == "glu":
        val = mm(0) + b_ref[:, :D_MODEL]
        gate = mm(1) + b_ref[:, D_MODEL:]
        outs[0][...] = val * _sigmoid(gate)
    else:
        for g in range(3):
            outs[g][...] = mm(g).astype(outs[g].dtype)


def _modmm(x2, mods, layer, mod_row, kind, w, j, b=None, kv_dtype=BF16):
    rows = x2.shape[0]
    n_cols = w.shape[2]
    row_spec = pl.BlockSpec((TM_MM, D_MODEL), lambda i: (i, 0))
    in_specs = [row_spec,
                pl.BlockSpec((None, 8, 6 * D_MODEL), lambda i: (layer, 0, 0)),
                pl.BlockSpec((None, D_MODEL, n_cols), lambda i: (j, 0, 0),
                             pipeline_mode=pl.Buffered(1))]
    args = [x2, mods, w]
    if kind == "lru":
        out_dtypes = [F32, F32]
    elif kind == "glu":
        in_specs.append(pl.BlockSpec((None, 1, n_cols), lambda i: (j, 0, 0)))
        args.append(b.reshape(b.shape[0], 1, n_cols))
        out_dtypes = [F32]
    else:
        out_dtypes = [BF16, kv_dtype, kv_dtype]
    return pl.pallas_call(
        functools.partial(_modmm_kernel, kind=kind, mod_row=mod_row),
        grid=(rows // TM_MM,),
        in_specs=in_specs,
        out_specs=[row_spec] * len(out_dtypes),
        out_shape=[jax.ShapeDtypeStruct((rows, D_MODEL), dt) for dt in out_dtypes],
        scratch_shapes=[pltpu.VMEM((D_MODEL, n_cols), BF16)],
        compiler_params=_cparams(("arbitrary",)),
        name="modmm_" + kind,
    )(*args)


def _to_time_major(x2):
    return jnp.swapaxes(x2.reshape(SLOTS, PS, CB), 0, 1)


def _from_time_major(x3):
    return jnp.swapaxes(x3, 0, 1).reshape(GROUP_ROWS, CB)


def _slot_iota():
    return lax.broadcasted_iota(jnp.int32, (SLOTS, CB), 0)


def _from_prev_slot(tile):
    return jnp.where(_slot_iota() == 0, 0.0, pltpu.roll(tile, 1, 0))


def _from_next_slot(tile):
    return jnp.where(_slot_iota() == SLOTS - 1, 0.0, pltpu.roll(tile, SLOTS - 1, 0))


def _fill_padded(pad_ref, x_ref, lo, hi, chunked):
    pad_ref[lo:lo + PS] = _to_time_major(x_ref[...])
    for r in range(lo):
        if chunked:
            pad_ref[r] = _from_prev_slot(pad_ref[PS + r])
        else:
            pad_ref[r] = jnp.zeros((SLOTS, CB), F32)
    for r in range(hi):
        if chunked:
            pad_ref[lo + PS + r] = _from_next_slot(pad_ref[lo + r])
        else:
            pad_ref[lo + PS + r] = jnp.zeros((SLOTS, CB), F32)


def _group_spec():
    return pl.BlockSpec((GROUP_ROWS, CB), lambda n, g: (g, n))


def _chan_spec(lead, j):
    return pl.BlockSpec((None, lead, CB), lambda n, g: (j, 0, n))


LRU_TC = 32


def _softplus(x):
    return jnp.maximum(x, 0.0) + jnp.log1p(jnp.exp(-jnp.abs(x)))


def _lru_seq_kernel(*refs, chunked):
    if chunked:
        (rec_ref, gate_ref, cw_ref, cb_ref, wa_ref, ba_ref, wx_ref, bx_ref, lam_ref, h0_ref,
         y_ref, pad_ref, af_ref, bf_ref, ab_ref, bb_ref, wbf_ref) = refs
    else:
        (rec_ref, gate_ref, cw_ref, cb_ref, wa_ref, ba_ref, wx_ref, bx_ref, lam_ref,
         y_ref, st_ref, pad_ref, af_ref, bf_ref, ab_ref, bb_ref, wbf_ref) = refs
    lo = (LRU_CONV_W - 1) // 2
    hi = LRU_CONV_W - 1 - lo
    _fill_padded(pad_ref, rec_ref, lo, hi, chunked)

    a_refs = (af_ref, ab_ref)
    b_refs = (bf_ref, bb_ref)
    hc = [-0.5 * LRU_C * _softplus(-lam_ref[d:d + 1, :]) for d in range(2)]
    for d in range(2):
        wbf_ref[2 * d] = (0.5 * wa_ref[d]).astype(BF16)
        wbf_ref[2 * d + 1] = (0.5 * wx_ref[d]).astype(BF16)
    hba = [0.5 * ba_ref[d:d + 1, :] for d in range(2)]
    hbx = [0.5 * bx_ref[d:d + 1, :] for d in range(2)]

    def gates(ci, carry):
        t0 = pl.multiple_of(ci * LRU_TC, LRU_TC)
        xf = cb_ref[...] + cw_ref[0:1, :] * pad_ref[pl.ds(t0, LRU_TC)]
        for k in range(1, LRU_CONV_W):
            xf = xf + cw_ref[k:k + 1, :] * pad_ref[pl.ds(t0 + k, LRU_TC)]
        x2 = xf.reshape(LRU_TC * SLOTS, CB)
        xb = x2.astype(BF16)
        hx = 0.5 * x2
        for d in range(2):
            tr = jnp.tanh(_dot(xb, wbf_ref[2 * d]) + hba[d])
            ti = jnp.tanh(_dot(xb, wbf_ref[2 * d + 1]) + hbx[d])
            log_a = hc[d] * tr + hc[d]
            a = jnp.exp(log_a)
            one_m_a2 = -jnp.tanh(log_a) * (a * a + 1.0)
            root = jnp.where(one_m_a2 > 0.0, one_m_a2 * lax.rsqrt(one_m_a2), 0.0)
            bx = root * (hx * ti + hx)
            a_refs[d][pl.ds(t0, LRU_TC)] = a.reshape(LRU_TC, SLOTS, CB)
            b_refs[d][pl.ds(t0, LRU_TC)] = bx.reshape(LRU_TC, SLOTS, CB)
        return carry

    lax.fori_loop(0, PS // LRU_TC, gates, 0)

    zero = jnp.zeros((SLOTS, CB), F32)
    one = jnp.ones((SLOTS, CB), F32)

    def scan(t, carry):
        hf, hb, pf, pb = carry
        tb = PS - 1 - t
        a = af_ref[t]
        hf = a * hf + bf_ref[t]
        bf_ref[t] = hf
        ar = ab_ref[tb]
        hb = ar * hb + bb_ref[tb]
        bb_ref[tb] = hb
        if chunked:
            pf = a * pf
            af_ref[t] = pf
            pb = ar * pb
            ab_ref[tb] = pb
        return hf, hb, pf, pb

    lax.fori_loop(0, PS, scan, (zero, zero, one, one), unroll=4)

    if chunked:
        slot = _slot_iota()
        h0f = jnp.broadcast_to(h0_ref[0:1, :], (SLOTS, CB))
        h0b = jnp.broadcast_to(h0_ref[1:2, :], (SLOTS, CB))
        end_f, prod_f = bf_ref[PS - 1], af_ref[PS - 1]
        end_b, prod_b = bb_ref[0], ab_ref[0]
        in_f = jnp.where(slot == 0, h0f, 0.0)
        in_b = jnp.where(slot == SLOTS - 1, h0b, 0.0)
        for _ in range(SLOTS - 1):
            in_f = jnp.where(slot == 0, h0f, pltpu.roll(end_f + prod_f * in_f, 1, 0))
            in_b = jnp.where(slot == SLOTS - 1, h0b,
                             pltpu.roll(end_b + prod_b * in_b, SLOTS - 1, 0))
    else:
        st_ref[0] = bf_ref[PS - 1]
        st_ref[1] = bb_ref[0]

    def combine(ci, carry):
        sl = pl.ds(pl.multiple_of(ci * LRU_TC, LRU_TC), LRU_TC)
        hs = bf_ref[sl] + bb_ref[sl]
        if chunked:
            hs = hs + af_ref[sl] * in_f + ab_ref[sl] * in_b
        bf_ref[sl] = hs
        return carry

    lax.fori_loop(0, PS // LRU_TC, combine, 0)
    y_ref[...] = (_from_time_major(bf_ref[...]) * gate_ref[...]).astype(y_ref.dtype)


def _lru_seq(rec, gate, p, j, state_lru=None):
    chunked = state_lru is not None
    rows = rec.shape[0]
    groups = rows // GROUP_ROWS
    n_layers = p["lru_conv_b"].shape[0]
    seq = _group_spec()
    wblk = pl.BlockSpec((None, 2, None, CB, CB), lambda n, g: (j, 0, n, 0, 0))
    in_specs = [seq, seq, _chan_spec(LRU_CONV_W, j), _chan_spec(1, j),
                wblk, _chan_spec(2, j), wblk, _chan_spec(2, j), _chan_spec(2, j)]
    args = [rec, gate, p["lru_conv_w"], p["lru_conv_b"].reshape(n_layers, 1, D_MODEL),
            p["lru_w_a"], p["lru_b_a"], p["lru_w_x"], p["lru_b_x"], p["lru_lambda"]]
    y_shape = jax.ShapeDtypeStruct((rows, D_MODEL), BF16)
    if chunked:
        in_specs.append(pl.BlockSpec((None, None, 2, CB), lambda n, g: (g, j, 0, n)))
        args.append(state_lru)
        out_specs = [seq]
        out_shape = [y_shape]
    else:
        out_specs = [seq, pl.BlockSpec((2, SLOTS, CB), lambda n, g: (0, g, n))]
        out_shape = [y_shape, jax.ShapeDtypeStruct((2, groups * SLOTS, D_MODEL), F32)]
    tile = (PS, SLOTS, CB)
    outs = pl.pallas_call(
        functools.partial(_lru_seq_kernel, chunked=chunked),
        grid=(NB, groups),
        in_specs=in_specs,
        out_specs=out_specs,
        out_shape=out_shape,
        scratch_shapes=[pltpu.VMEM((PS + LRU_CONV_W - 1, SLOTS, CB), F32)]
        + [pltpu.VMEM(tile, F32) for _ in range(4)] + [pltpu.VMEM((4, CB, CB), BF16)],
        compiler_params=_cparams(("parallel", "parallel")),
        name="lru_seq_dec" if chunked else "lru_seq_ctx",
    )(*args)
    return (outs[0], None) if chunked else (outs[0], outs[1])


CONV_TC = 16


def _conv_seq_kernel(z_ref, w_ref, b_ref, o_ref, pad_ref, out_ref, *, chunked):
    lo = (CONF_CONV_W - 1) // 2
    hi = CONF_CONV_W - 1 - lo
    _fill_padded(pad_ref, z_ref, lo, hi, chunked)

    def chunk(ci, carry):
        t0 = pl.multiple_of(ci * CONV_TC, CONV_TC)
        acc = b_ref[...] + w_ref[0:1, :] * pad_ref[pl.ds(t0, CONV_TC)]
        for k in range(1, CONF_CONV_W):
            acc = acc + w_ref[k:k + 1, :] * pad_ref[pl.ds(t0 + k, CONV_TC)]
        out_ref[pl.ds(t0, CONV_TC)] = acc
        return carry

    lax.fori_loop(0, PS // CONV_TC, chunk, 0)
    o_ref[...] = _from_time_major(out_ref[...])


def _conv_seq(z, w, b, j, chunked):
    rows = z.shape[0]
    seq = _group_spec()
    return pl.pallas_call(
        functools.partial(_conv_seq_kernel, chunked=chunked),
        grid=(NB, rows // GROUP_ROWS),
        in_specs=[seq, _chan_spec(CONF_CONV_W, j), _chan_spec(1, j)],
        out_specs=seq,
        out_shape=jax.ShapeDtypeStruct((rows, D_MODEL), F32),
        scratch_shapes=[pltpu.VMEM((PS + CONF_CONV_W - 1, SLOTS, CB), F32),
                        pltpu.VMEM((PS, SLOTS, CB), F32)],
        compiler_params=_cparams(("parallel", "parallel")),
        name="conv_seq_dec" if chunked else "conv_seq_ctx",
    )(z, w, b.reshape(b.shape[0], 1, D_MODEL))


def _attend(q, parts):
    lane = lax.broadcasted_iota(jnp.int32, (1, LANES), 1)
    outs = []
    for par in range(2):
        sel = (lane < NA_HEAD_DIM) if par == 0 else (lane >= NA_HEAD_DIM)
        qm = jnp.where(sel, q, jnp.zeros_like(q)) * ATT_SCALE
        scores = []
        for k, _, bias in parts:
            s = _dot_t(qm, k)
            if bias is not None:
                s = s + bias[0][bias[1] + par]
            scores.append(s)
        mx = scores[0].max(axis=-1, keepdims=True)
        for s in scores[1:]:
            mx = jnp.maximum(mx, s.max(axis=-1, keepdims=True))
        den = 0.0
        acc = 0.0
        for s, (_, v, _) in zip(scores, parts):
            pr = jnp.exp(s - mx)
            den = den + pr.sum(axis=-1, keepdims=True)
            acc = acc + _dot(pr.astype(BF16), v)
        outs.append(acc / den)
    return jnp.where(lane < NA_HEAD_DIM, outs[0], outs[1])


def _attn_ctx_kernel(q_ref, k_ref, v_ref, o_ref):
    for s in range(D_MODEL // LANES):
        sl = slice(s * LANES, (s + 1) * LANES)
        k = k_ref[:, sl].astype(BF16)
        v = v_ref[:, sl].astype(BF16)
        o_ref[:, sl] = _attend(q_ref[:, sl], [(k, v, None)]).astype(o_ref.dtype)


def _attn_ctx(q, k, v, batch):
    blk = pl.BlockSpec((PS, D_MODEL), lambda b: (b, 0))
    return pl.pallas_call(
        _attn_ctx_kernel,
        grid=(batch,),
        in_specs=[blk, blk, blk],
        out_specs=blk,
        out_shape=jax.ShapeDtypeStruct(q.shape, BF16),
        compiler_params=_cparams(("parallel",)),
        name="attn_ctx",
    )(q, k, v)


def _attn_dec_kernel(q_ref, k_ref, v_ref, kc_ref, vc_ref, bias_ref, o_ref):
    i = pl.program_id(2)
    row0 = jnp.clip(Q_ROWS * i - NA_WIN_ROWS // 2, 0, GRID_H - WIN_ROWS_BLK)
    win = pl.ds(pl.multiple_of(row0 * GRID_W, GRID_W), WIN_KEYS)
    for s in range(DEC_SLABS):
        sl = slice(s * LANES, (s + 1) * LANES)
        kc = kc_ref[:, sl].astype(BF16)
        vc = vc_ref[:, sl].astype(BF16)
        o = _attend(q_ref[:, sl], [(k_ref[win, sl], v_ref[win, sl], (bias_ref, 2 * s)),
                                   (kc, vc, None)])
        o_ref[:, sl] = o.astype(o_ref.dtype)


def _qblk_window(i):
    row0 = min(max(Q_ROWS * i - NA_WIN_ROWS // 2, 0), GRID_H - WIN_ROWS_BLK)
    out = []
    for a in range(Q_ROWS):
        r = Q_ROWS * i + a
        rs = min(max(r - NA_WIN_ROWS // 2, 0), GRID_H - NA_WIN_ROWS)
        out.append((r, [rs <= row0 + w < rs + NA_WIN_ROWS for w in range(WIN_ROWS_BLK)]))
    return row0, out


BIAS_CLASSES = (0, 1, N_QBLK - 1)


def _bias_kernel(rpb_ref, o_ref):
    c = lax.broadcasted_iota(jnp.int32, (GRID_W, LANES), 0)
    l = lax.broadcasted_iota(jnp.int32, (GRID_W, LANES), 1)
    cs = jnp.clip(c - NA_WIN_COLS // 2, 0, GRID_W - NA_WIN_COLS)
    in_cols = (l >= cs) & (l < cs + NA_WIN_COLS)
    neg = jnp.full((GRID_W, LANES), NEG_BIG, F32)
    lo_half, hi_half = [], []
    for dr in range(2 * NA_WIN_ROWS - 1):
        row = jnp.broadcast_to(rpb_ref[dr:dr + 1, :], (GRID_W, LANES))
        t = pltpu.roll(row, LANES - (NA_WIN_COLS - 1), 1, stride=1, stride_axis=0)
        t = jnp.where(in_cols, t, NEG_BIG)
        lo_half.append(t)
        hi_half.append(pltpu.roll(t, GRID_W, 1))
    for cls, i in enumerate(BIAS_CLASSES):
        row0, qrows = _qblk_window(i)
        for a, (r, valid) in enumerate(qrows):
            for wp in range(WIN_ROWS_BLK // 2):
                halves = []
                for half, bank in enumerate((lo_half, hi_half)):
                    w = 2 * wp + half
                    halves.append(bank[row0 + w - r + NA_WIN_ROWS - 1] if valid[w] else neg)
                o_ref[cls, a * GRID_W:(a + 1) * GRID_W, wp * LANES:(wp + 1) * LANES] = (
                    jnp.where(l < GRID_W, halves[0], halves[1]))


def _attn_bias_table(rpb, j):
    nl, nh, ndr, ndc = rpb.shape
    rpb_p = jnp.pad(rpb, ((0, 0), (0, 0), (0, 16 - ndr), (0, LANES - ndc)))
    return pl.pallas_call(
        _bias_kernel,
        grid=(nh,),
        in_specs=[pl.BlockSpec((None, None, 16, LANES), lambda h: (j, h, 0, 0))],
        out_specs=pl.BlockSpec((len(BIAS_CLASSES), None, Q_BLK, WIN_KEYS), lambda h: (0, h, 0, 0)),
        out_shape=jax.ShapeDtypeStruct((len(BIAS_CLASSES), nh, Q_BLK, WIN_KEYS), F32),
        compiler_params=_cparams(("parallel",)),
        name="attn_bias",
    )(rpb_p)


def _attn_dec(q, k, v, kc, vc, j, bias, batch):
    t = GRID_H * GRID_W
    width = DEC_SLABS * LANES
    qblk = pl.BlockSpec((Q_BLK, width), lambda b, s, i: (b * N_QBLK + i, s))
    kvblk = pl.BlockSpec((None, t, width), lambda b, s, i: (b, 0, s))
    cblk = pl.BlockSpec((None, None, PS, width), lambda b, s, i: (b, j, 0, s))

    def bias_idx(b, s, i):
        cls = (i > 0).astype(jnp.int32) + (i == N_QBLK - 1).astype(jnp.int32)
        return (cls, s, 0, 0)

    return pl.pallas_call(
        _attn_dec_kernel,
        grid=(batch, D_MODEL // width, N_QBLK),
        in_specs=[qblk, kvblk, kvblk, cblk, cblk,
                  pl.BlockSpec((None, 2 * DEC_SLABS, Q_BLK, WIN_KEYS), bias_idx)],
        out_specs=qblk,
        out_shape=jax.ShapeDtypeStruct(q.shape, BF16),
        compiler_params=_cparams(("parallel", "parallel", "arbitrary")),
        name="attn_dec",
    )(q, k.reshape(batch, t, D_MODEL), v.reshape(batch, t, D_MODEL), kc, vc, bias)


def _pffn_kernel(*refs, conf, final, mod_row):
    refs = list(refs)
    x_ref, y_ref, mod_ref, wp_ref = refs[:4]
    pos = 4
    bp_ref = lng_ref = lnb_ref = fin_ref = None
    if conf:
        bp_ref, lng_ref, lnb_ref = refs[pos:pos + 3]
        pos += 3
    w1_ref, w2_ref = refs[pos:pos + 2]
    pos += 2
    if final:
        fin_ref = refs[pos]
        pos += 1
    o_ref, h2_ref, acc_ref = refs[pos:pos + 3]
    f = pl.program_id(1)

    @pl.when(f == 0)
    def _():
        _, _, g1, sh2, sc2, _ = _mod_parts(mod_ref, mod_row)
        if conf:
            z = y_ref[...]
            mu = jnp.mean(z, axis=-1, keepdims=True)
            zc = z - mu
            var = jnp.mean(zc * zc, axis=-1, keepdims=True)
            zn = zc * lax.rsqrt(var + EPS) * lng_ref[...] + lnb_ref[...]
            y = (zn * _sigmoid(zn)).astype(BF16)
        else:
            y = y_ref[...]
        proj = _dot(y, wp_ref[...].astype(BF16))
        if bp_ref is not None:
            proj = proj + bp_ref[...]
        x1 = x_ref[...] + g1 * proj
        o_ref[...] = x1
        h2_ref[...] = (_rms(x1) * (1.0 + sc2) + sh2).astype(BF16)
        acc_ref[...] = jnp.zeros_like(acc_ref)

    u = _dot(h2_ref[...], w1_ref[...])
    u = jnp.square(jnp.maximum(u, 0.0)).astype(BF16)
    acc_ref[...] += _dot(u, w2_ref[...])

    @pl.when(f == pl.num_programs(1) - 1)
    def _():
        g2 = _mod_parts(mod_ref, mod_row)[5]
        out = o_ref[...] + g2 * acc_ref[...]
        if final:
            out = _rms(out) * fin_ref[...]
        o_ref[...] = out


def _pffn(x2, y, mods, layer, mod_row, w_proj, j, w1, w2, b_proj=None, ln_g=None, ln_b=None,
          final_g=None):
    rows = x2.shape[0]
    conf = ln_g is not None
    final = final_g is not None
    row_spec = pl.BlockSpec((TM, D_MODEL), lambda i, f: (i, 0))
    vec_spec = pl.BlockSpec((None, 1, D_MODEL), lambda i, f: (j, 0, 0))
    in_specs = [row_spec, row_spec,
                pl.BlockSpec((None, 8, 6 * D_MODEL), lambda i, f: (layer, 0, 0)),
                pl.BlockSpec((None, D_MODEL, D_MODEL), lambda i, f: (j, 0, 0),
                             pipeline_mode=pl.Buffered(1))]
    args = [x2, y, mods, w_proj]
    if conf:
        in_specs += [vec_spec, vec_spec, vec_spec]
        args += [v.reshape(v.shape[0], 1, D_MODEL) for v in (b_proj, ln_g, ln_b)]
    in_specs += [pl.BlockSpec((None, D_MODEL, FK), lambda i, f: (layer, 0, f)),
                 pl.BlockSpec((None, FK, D_MODEL), lambda i, f: (layer, f, 0))]
    args += [w1, w2]
    if final:
        in_specs.append(pl.BlockSpec((1, D_MODEL), lambda i, f: (0, 0)))
        args.append(final_g.reshape(1, D_MODEL))
    return pl.pallas_call(
        functools.partial(_pffn_kernel, conf=conf, final=final, mod_row=mod_row),
        grid=(rows // TM, D_FF // FK),
        in_specs=in_specs,
        out_specs=row_spec,
        out_shape=jax.ShapeDtypeStruct((rows, D_MODEL), F32),
        scratch_shapes=[pltpu.VMEM((TM, D_MODEL), BF16), pltpu.VMEM((TM, D_MODEL), F32)],
        compiler_params=_cparams(("parallel", "arbitrary")),
        name="pffn" + ("_conf" if conf else "") + ("_final" if final else ""),
    )(*args)


def _trunk(x, mods, is_ctx, p, state_lru, cache_k, cache_v, bias_tabs):
    bsz, t, d = x.shape
    rows = bsz * t
    x2 = x.reshape(rows, d)
    depth = mods.shape[0]
    row_mm = _make_mod_row(is_ctx, t, TM_MM)
    row_ffn = _make_mod_row(is_ctx, t, TM)
    states, ks, vs = [], [], []
    for i in range(depth):
        kind, j = i % 3, i // 3
        fin = p["final_g"] if i == depth - 1 else None
        ffn = functools.partial(_pffn, x2, mods=mods, layer=i, mod_row=row_ffn, j=j,
                                w1=p["w_ff1"], w2=p["w_ff2"], final_g=fin)
        if kind == 0:
            gate, rec = _modmm(x2, mods, i, row_mm, "lru", p["lru_w_in"], j)
            y, st = _lru_seq(rec, gate, p, j, None if is_ctx else state_lru)
            if is_ctx:
                states.append(st)
            x2 = ffn(y=y, w_proj=p["lru_w_out"])
        elif kind == 1:
            (z,) = _modmm(x2, mods, i, row_mm, "glu", p["conf_w_pw1"], j, b=p["conf_b_pw1"])
            zc = _conv_seq(z, p["conf_dw_w"], p["conf_dw_b"], j, chunked=not is_ctx)
            x2 = ffn(y=zc, w_proj=p["conf_w_pw2"], b_proj=p["conf_b_pw2"],
                     ln_g=p["conf_ln_g"], ln_b=p["conf_ln_b"])
        else:
            if is_ctx:
                q, k, v = _modmm(x2, mods, i, row_mm, "qkv", p["na_w_qkv"], j, kv_dtype=F32)
                o = _attn_ctx(q, k, v, bsz)
                ks.append(k.reshape(bsz, t, NA_HEADS, NA_HEAD_DIM))
                vs.append(v.reshape(bsz, t, NA_HEADS, NA_HEAD_DIM))
            else:
                q, k, v = _modmm(x2, mods, i, row_mm, "qkv", p["na_w_qkv"], j, kv_dtype=BF16)
                kc = cache_k.reshape(cache_k.shape[:3] + (d,))
                vc = cache_v.reshape(cache_v.shape[:3] + (d,))
                o = _attn_dec(q, k, v, kc, vc, j, bias_tabs[j], bsz)
            x2 = ffn(y=o, w_proj=p["na_w_o"])
    return x2.reshape(bsz, t, d), states, ks, vs


def kernel(x_prompt, x_sample, state_lru, cache_k, cache_v, c, c_ctx, w_mod, b_mod, w_ff1, w_ff2, lru_w_in, lru_conv_w, lru_conv_b, lru_w_a, lru_b_a, lru_w_x, lru_b_x, lru_lambda, lru_w_out, conf_w_pw1, conf_b_pw1, conf_dw_w, conf_dw_b, conf_ln_g, conf_ln_b, conf_w_pw2, conf_b_pw2, na_w_qkv, na_w_o, na_rpb, final_g):
    p = dict(w_ff1=_to_bf16(w_ff1), w_ff2=_to_bf16(w_ff2), lru_w_in=lru_w_in, lru_conv_w=lru_conv_w,
             lru_conv_b=lru_conv_b, lru_w_a=lru_w_a, lru_b_a=lru_b_a, lru_w_x=lru_w_x,
             lru_b_x=lru_b_x, lru_lambda=lru_lambda, lru_w_out=lru_w_out,
             conf_w_pw1=conf_w_pw1, conf_b_pw1=conf_b_pw1, conf_dw_w=conf_dw_w,
             conf_dw_b=conf_dw_b, conf_ln_g=conf_ln_g, conf_ln_b=conf_ln_b,
             conf_w_pw2=conf_w_pw2, conf_b_pw2=conf_b_pw2, na_w_qkv=na_w_qkv, na_w_o=na_w_o,
             final_g=final_g)
    dec_b = c.shape[0]
    assert 1 + dec_b <= 8
    cond8 = jnp.concatenate([c_ctx[None, :], c, jnp.zeros((8 - 1 - dec_b, D_MODEL), F32)], axis=0)
    mods = _adaln(cond8, w_mod, b_mod)
    bias_tabs = [_attn_bias_table(na_rpb, j) for j in range(na_rpb.shape[0])]

    y_prompt, states, ks, vs = _trunk(x_prompt, mods, True, p, None, None, None, None)
    y_sample, _, _, _ = _trunk(x_sample, mods, False, p, state_lru, cache_k, cache_v, bias_tabs)

    new_state = jnp.stack([jnp.transpose(s, (1, 0, 2)) for s in states], axis=1)
    new_k = jnp.stack(ks, axis=1)
    new_v = jnp.stack(vs, axis=1)
    return (y_prompt, y_sample, new_state, new_k, new_v)
```

```python
import functools
import math

import jax
import jax.numpy as jnp
from jax import lax
from jax.experimental import pallas as pl
from jax.experimental.pallas import tpu as pltpu

F32 = jnp.float32
BF16 = jnp.bfloat16

D_MODEL = 1024
D_FF = 4 * D_MODEL
PS = 256
CB = 256
NB = D_MODEL // CB
SLOTS = 8
GROUP_ROWS = SLOTS * PS
GRID_W = 64
GRID_H = 32
NA_HEADS = 16
NA_HEAD_DIM = 64
NA_WIN_ROWS = 8
NA_WIN_COLS = 16
ATT_SCALE = NA_HEAD_DIM ** -0.5
assert math.frexp(ATT_SCALE)[0] == 0.5, "the attention kernels scale bf16 queries exactly"
LRU_C = 8.0
LRU_CONV_W = 4
CONF_CONV_W = 31
EPS = 1e-6
NEG_BIG = -1e30

LANES = 128
TM = 1024
TM_MM = 512
FK = 1024
CAST_TILE = 1024
MOD_TILE = 1536
Q_ROWS = 4
Q_BLK = Q_ROWS * GRID_W
N_QBLK = GRID_H // Q_ROWS
WIN_ROWS_BLK = 12
WIN_KEYS = WIN_ROWS_BLK * GRID_W
DEC_SLABS = 8
VMEM_LIMIT = 56 * 1024 * 1024


def _cparams(sem):
    return pltpu.CompilerParams(dimension_semantics=sem, vmem_limit_bytes=VMEM_LIMIT)


def _dot(a, b):
    return jnp.dot(a, b, preferred_element_type=F32)


def _dot_t(a, b):
    return lax.dot_general(a, b, (((1,), (1,)), ((), ())), preferred_element_type=F32)


def _rms(x):
    return x * lax.rsqrt(jnp.mean(x * x, axis=-1, keepdims=True) + EPS)


def _sigmoid(x):
    return 0.5 * jnp.tanh(0.5 * x) + 0.5


def _mod_parts(mod_ref, mod_row):
    m = mod_ref[pl.ds(mod_row(pl.program_id(0)), 1), :]
    return [m[:, k * D_MODEL:(k + 1) * D_MODEL] for k in range(6)]


def _make_mod_row(is_ctx, seq_len, tile_rows):
    if is_ctx:
        return lambda i: 0
    return lambda i: 1 + (i * tile_rows) // seq_len


def _adaln_kernel(c_ref, w_ref, b_ref, o_ref):
    c = c_ref[...]
    s = (c * jax.nn.sigmoid(c)).astype(BF16)
    o_ref[...] = _dot(s, w_ref[...].astype(BF16)) + b_ref[...]


def _adaln(cond8, w_mod, b_mod):
    depth = w_mod.shape[0]
    n_out = w_mod.shape[2]
    return pl.pallas_call(
        _adaln_kernel,
        grid=(depth, n_out // MOD_TILE),
        in_specs=[
            pl.BlockSpec((8, D_MODEL), lambda l, n: (0, 0)),
            pl.BlockSpec((None, D_MODEL, MOD_TILE), lambda l, n: (l, 0, n)),
            pl.BlockSpec((None, 1, MOD_TILE), lambda l, n: (l, 0, n)),
        ],
        out_specs=pl.BlockSpec((None, 8, MOD_TILE), lambda l, n: (l, 0, n)),
        out_shape=jax.ShapeDtypeStruct((depth, 8, n_out), F32),
        compiler_params=_cparams(("parallel", "parallel")),
        name="adaln",
    )(cond8, w_mod, b_mod.reshape(depth, 1, n_out))


def _cast_kernel(w_ref, o_ref):
    o_ref[...] = w_ref[...].astype(o_ref.dtype)


def _to_bf16(w):
    nl, a, b = w.shape
    blk = pl.BlockSpec((None, CAST_TILE, CAST_TILE), lambda l, i, k: (l, i, k))
    return pl.pallas_call(
        _cast_kernel,
        grid=(nl, a // CAST_TILE, b // CAST_TILE),
        in_specs=[blk],
        out_specs=blk,
        out_shape=jax.ShapeDtypeStruct(w.shape, BF16),
        compiler_params=_cparams(("parallel", "parallel", "parallel")),
        name="cast_bf16",
    )(w)


def _modulated(x_ref, mod_ref, mod_row):
    sh1, sc1 = _mod_parts(mod_ref, mod_row)[:2]
    return (_rms(x_ref[...]) * (1.0 + sc1) + sh1).astype(BF16)


def _qkv_kernel(x_ref, mod_ref, w_ref, *refs, mod_row, with_cache):
    qkv_refs, wbf_ref = refs[:3], refs[-1]

    @pl.when(pl.program_id(0) == 0)
    def _():
        wbf_ref[...] = w_ref[...].astype(BF16)

    h = _modulated(x_ref, mod_ref, mod_row)
    for g, o_ref in enumerate(qkv_refs):
        o = _dot(h, wbf_ref[:, g * D_MODEL:(g + 1) * D_MODEL])
        o_ref[...] = o.astype(o_ref.dtype)
        if with_cache and g > 0:
            refs[2 + g][...] = o.reshape(TM_MM, NA_HEADS, NA_HEAD_DIM)


def _qkv(x2, mods, layer, mod_row, w, j, with_cache):
    rows = x2.shape[0]
    n_cols = w.shape[2]
    row_spec = pl.BlockSpec((TM_MM, D_MODEL), lambda i: (i, 0))
    out_specs = [row_spec] * 3
    out_shape = [jax.ShapeDtypeStruct((rows, D_MODEL), BF16)] * 3
    if with_cache:
        out_specs += [pl.BlockSpec((TM_MM, NA_HEADS, NA_HEAD_DIM), lambda i: (i, 0, 0))] * 2
        out_shape += [jax.ShapeDtypeStruct((rows, NA_HEADS, NA_HEAD_DIM), F32)] * 2
    return pl.pallas_call(
        functools.partial(_qkv_kernel, mod_row=mod_row, with_cache=with_cache),
        grid=(rows // TM_MM,),
        in_specs=[row_spec,
                  pl.BlockSpec((None, 8, 6 * D_MODEL), lambda i: (layer, 0, 0)),
                  pl.BlockSpec((None, D_MODEL, n_cols), lambda i: (j, 0, 0),
                               pipeline_mode=pl.Buffered(1))],
        out_specs=out_specs,
        out_shape=out_shape,
        scratch_shapes=[pltpu.VMEM((D_MODEL, n_cols), BF16)],
        compiler_params=_cparams(("arbitrary",)),
        name="modmm_qkv",
    )(x2, mods, w)


def _to_time_major(x2):
    return jnp.swapaxes(x2.reshape(SLOTS, PS, CB), 0, 1)


def _from_time_major(x3):
    return jnp.swapaxes(x3, 0, 1).reshape(GROUP_ROWS, CB)


def _slot_iota():
    return lax.broadcasted_iota(jnp.int32, (SLOTS, CB), 0)


def _from_prev_slot(tile):
    return jnp.where(_slot_iota() == 0, 0.0, pltpu.roll(tile, 1, 0))


def _from_next_slot(tile):
    return jnp.where(_slot_iota() == SLOTS - 1, 0.0, pltpu.roll(tile, SLOTS - 1, 0))


def _fill_padded(pad_ref, x, lo, hi, chunked):
    pad_ref[lo:lo + PS] = _to_time_major(x)
    for r in range(lo):
        if chunked:
            pad_ref[r] = _from_prev_slot(pad_ref[PS + r])
        else:
            pad_ref[r] = jnp.zeros((SLOTS, CB), F32)
    for r in range(hi):
        if chunked:
            pad_ref[lo + PS + r] = _from_next_slot(pad_ref[lo + r])
        else:
            pad_ref[lo + PS + r] = jnp.zeros((SLOTS, CB), F32)


def _group_spec():
    return pl.BlockSpec((GROUP_ROWS, CB), lambda g, n: (g, n))


def _chan_spec(lead, j, col0=0):
    return pl.BlockSpec((None, lead, CB), lambda g, n: (j, 0, col0 + n))


def _seq_in_specs(layer):
    return [pl.BlockSpec((GROUP_ROWS, D_MODEL), lambda g, n: (g, 0)),
            pl.BlockSpec((None, 8, 6 * D_MODEL), lambda g, n: (layer, 0, 0))]


def _seq_modulated(x_ref, mod_ref, h_ref, mod_row):
    @pl.when(pl.program_id(1) == 0)
    def _():
        h_ref[...] = _modulated(x_ref, mod_ref, mod_row)


LRU_TC = 32


def _softplus(x):
    return jnp.maximum(x, 0.0) + jnp.log1p(jnp.exp(-jnp.abs(x)))


def _lru_seq_kernel(*refs, chunked, mod_row):
    if chunked:
        (x_ref, mod_ref, wg_ref, wr_ref, cw_ref, cb_ref, wa_ref, ba_ref, wx_ref, bx_ref, lam_ref,
         h0_ref, y_ref, h_ref, gate_ref, pad_ref, af_ref, bf_ref, ab_ref, bb_ref, wbf_ref) = refs
    else:
        (x_ref, mod_ref, wg_ref, wr_ref, cw_ref, cb_ref, wa_ref, ba_ref, wx_ref, bx_ref, lam_ref,
         y_ref, st_ref, h_ref, gate_ref, pad_ref, af_ref, bf_ref, ab_ref, bb_ref, wbf_ref) = refs
    lo = (LRU_CONV_W - 1) // 2
    hi = LRU_CONV_W - 1 - lo
    _seq_modulated(x_ref, mod_ref, h_ref, mod_row)
    h = h_ref[...]
    gate_ref[...] = jax.nn.gelu(_dot(h, wg_ref[...].astype(BF16)), approximate=True)
    _fill_padded(pad_ref, _dot(h, wr_ref[...].astype(BF16)), lo, hi, chunked)

    a_refs = (af_ref, ab_ref)
    b_refs = (bf_ref, bb_ref)
    hc = [-0.5 * LRU_C * _softplus(-lam_ref[d:d + 1, :]) for d in range(2)]
    for d in range(2):
        wbf_ref[2 * d] = (0.5 * wa_ref[d]).astype(BF16)
        wbf_ref[2 * d + 1] = (0.5 * wx_ref[d]).astype(BF16)
    hba = [0.5 * ba_ref[d:d + 1, :] for d in range(2)]
    hbx = [0.5 * bx_ref[d:d + 1, :] for d in range(2)]

    def gates(ci, carry):
        t0 = pl.multiple_of(ci * LRU_TC, LRU_TC)
        xf = cb_ref[...] + cw_ref[0:1, :] * pad_ref[pl.ds(t0, LRU_TC)]
        for k in range(1, LRU_CONV_W):
            xf = xf + cw_ref[k:k + 1, :] * pad_ref[pl.ds(t0 + k, LRU_TC)]
        x2 = xf.reshape(LRU_TC * SLOTS, CB)
        xb = x2.astype(BF16)
        hx = 0.5 * x2
        for d in range(2):
            tr = jnp.tanh(_dot(xb, wbf_ref[2 * d]) + hba[d])
            ti = jnp.tanh(_dot(xb, wbf_ref[2 * d + 1]) + hbx[d])
            log_a = hc[d] * tr + hc[d]
            a = jnp.exp(log_a)
            one_m_a2 = -jnp.tanh(log_a) * (a * a + 1.0)
            root = jnp.where(one_m_a2 > 0.0, one_m_a2 * lax.rsqrt(one_m_a2), 0.0)
            bx = root * (hx * ti + hx)
            a_refs[d][pl.ds(t0, LRU_TC)] = a.reshape(LRU_TC, SLOTS, CB)
            b_refs[d][pl.ds(t0, LRU_TC)] = bx.reshape(LRU_TC, SLOTS, CB)
        return carry

    lax.fori_loop(0, PS // LRU_TC, gates, 0)

    zero = jnp.zeros((SLOTS, CB), F32)
    one = jnp.ones((SLOTS, CB), F32)

    def scan(t, carry):
        hf, hb, pf, pb = carry
        tb = PS - 1 - t
        a = af_ref[t]
        hf = a * hf + bf_ref[t]
        bf_ref[t] = hf
        ar = ab_ref[tb]
        hb = ar * hb + bb_ref[tb]
        bb_ref[tb] = hb
        if chunked:
            pf = a * pf
            af_ref[t] = pf
            pb = ar * pb
            ab_ref[tb] = pb
        return hf, hb, pf, pb

    lax.fori_loop(0, PS, scan, (zero, zero, one, one), unroll=4)

    if chunked:
        slot = _slot_iota()
        h0f = jnp.broadcast_to(h0_ref[0:1, :], (SLOTS, CB))
        h0b = jnp.broadcast_to(h0_ref[1:2, :], (SLOTS, CB))
        end_f, prod_f = bf_ref[PS - 1], af_ref[PS - 1]
        end_b, prod_b = bb_ref[0], ab_ref[0]
        in_f = jnp.where(slot == 0, h0f, 0.0)
        in_b = jnp.where(slot == SLOTS - 1, h0b, 0.0)
        for _ in range(SLOTS - 1):
            in_f = jnp.where(slot == 0, h0f, pltpu.roll(end_f + prod_f * in_f, 1, 0))
            in_b = jnp.where(slot == SLOTS - 1, h0b,
                             pltpu.roll(end_b + prod_b * in_b, SLOTS - 1, 0))
    else:
        st_ref[0] = bf_ref[PS - 1]
        st_ref[1] = bb_ref[0]

    def combine(ci, carry):
        sl = pl.ds(pl.multiple_of(ci * LRU_TC, LRU_TC), LRU_TC)
        hs = bf_ref[sl] + bb_ref[sl]
        if chunked:
            hs = hs + af_ref[sl] * in_f + ab_ref[sl] * in_b
        bf_ref[sl] = hs
        return carry

    lax.fori_loop(0, PS // LRU_TC, combine, 0)
    y_ref[...] = (_from_time_major(bf_ref[...]) * gate_ref[...]).astype(y_ref.dtype)


def _lru_seq(x2, mods, layer, mod_row, p, j, state_lru=None):
    chunked = state_lru is not None
    rows = x2.shape[0]
    groups = rows // GROUP_ROWS
    n_layers = p["lru_conv_b"].shape[0]
    seq = _group_spec()
    wblk = pl.BlockSpec((None, 2, None, CB, CB), lambda g, n: (j, 0, n, 0, 0))
    in_specs = _seq_in_specs(layer) + [
        _chan_spec(D_MODEL, j), _chan_spec(D_MODEL, j, NB),
        _chan_spec(LRU_CONV_W, j), _chan_spec(1, j),
        wblk, _chan_spec(2, j), wblk, _chan_spec(2, j), _chan_spec(2, j)]
    args = [x2, mods] + [p["lru_w_in"]] * 2 + [
            p["lru_conv_w"], p["lru_conv_b"].reshape(n_layers, 1, D_MODEL),
            p["lru_w_a"], p["lru_b_a"], p["lru_w_x"], p["lru_b_x"], p["lru_lambda"]]
    y_shape = jax.ShapeDtypeStruct((rows, D_MODEL), BF16)
    if chunked:
        in_specs.append(pl.BlockSpec((None, None, 2, CB), lambda g, n: (g, j, 0, n)))
        args.append(state_lru)
        out_specs = [seq]
        out_shape = [y_shape]
    else:
        out_specs = [seq, pl.BlockSpec((2, SLOTS, CB), lambda g, n: (0, g, n))]
        out_shape = [y_shape, jax.ShapeDtypeStruct((2, groups * SLOTS, D_MODEL), F32)]
    tile = (PS, SLOTS, CB)
    outs = pl.pallas_call(
        functools.partial(_lru_seq_kernel, chunked=chunked, mod_row=mod_row),
        grid=(groups, NB),
        in_specs=in_specs,
        out_specs=out_specs,
        out_shape=out_shape,
        scratch_shapes=[pltpu.VMEM((GROUP_ROWS, D_MODEL), BF16), pltpu.VMEM((GROUP_ROWS, CB), F32),
                        pltpu.VMEM((PS + LRU_CONV_W - 1, SLOTS, CB), F32)]
        + [pltpu.VMEM(tile, F32) for _ in range(4)] + [pltpu.VMEM((4, CB, CB), BF16)],
        compiler_params=_cparams(("parallel", "arbitrary")),
        name="lru_seq_dec" if chunked else "lru_seq_ctx",
    )(*args)
    return (outs[0], None) if chunked else (outs[0], outs[1])


CONV_TC = 16


def _conv_seq_kernel(x_ref, mod_ref, wv_ref, wg_ref, bv_ref, bg_ref, w_ref, b_ref, o_ref,
                     h_ref, pad_ref, out_ref, *, chunked, mod_row):
    lo = (CONF_CONV_W - 1) // 2
    hi = CONF_CONV_W - 1 - lo
    _seq_modulated(x_ref, mod_ref, h_ref, mod_row)
    h = h_ref[...]
    val = _dot(h, wv_ref[...].astype(BF16)) + bv_ref[...]
    gate = _dot(h, wg_ref[...].astype(BF16)) + bg_ref[...]
    _fill_padded(pad_ref, val * _sigmoid(gate), lo, hi, chunked)

    def chunk(ci, carry):
        t0 = pl.multiple_of(ci * CONV_TC, CONV_TC)
        acc = b_ref[...] + w_ref[0:1, :] * pad_ref[pl.ds(t0, CONV_TC)]
        for k in range(1, CONF_CONV_W):
            acc = acc + w_ref[k:k + 1, :] * pad_ref[pl.ds(t0 + k, CONV_TC)]
        out_ref[pl.ds(t0, CONV_TC)] = acc
        return carry

    lax.fori_loop(0, PS // CONV_TC, chunk, 0)
    o_ref[...] = _from_time_major(out_ref[...])


def _conv_seq(x2, mods, layer, mod_row, p, j, chunked):
    rows = x2.shape[0]
    n_layers = p["conf_dw_b"].shape[0]
    w_pw1 = p["conf_w_pw1"]
    b_pw1 = p["conf_b_pw1"].reshape(n_layers, 1, 2 * D_MODEL)
    return pl.pallas_call(
        functools.partial(_conv_seq_kernel, chunked=chunked, mod_row=mod_row),
        grid=(rows // GROUP_ROWS, NB),
        in_specs=_seq_in_specs(layer) + [
            _chan_spec(D_MODEL, j), _chan_spec(D_MODEL, j, NB),
            _chan_spec(1, j), _chan_spec(1, j, NB),
            _chan_spec(CONF_CONV_W, j), _chan_spec(1, j)],
        out_specs=_group_spec(),
        out_shape=jax.ShapeDtypeStruct((rows, D_MODEL), F32),
        scratch_shapes=[pltpu.VMEM((GROUP_ROWS, D_MODEL), BF16),
                        pltpu.VMEM((PS + CONF_CONV_W - 1, SLOTS, CB), F32),
                        pltpu.VMEM((PS, SLOTS, CB), F32)],
        compiler_params=_cparams(("parallel", "arbitrary")),
        name="conv_seq_dec" if chunked else "conv_seq_ctx",
    )(x2, mods, w_pw1, w_pw1, b_pw1, b_pw1, p["conf_dw_w"],
      p["conf_dw_b"].reshape(n_layers, 1, D_MODEL))


def _attend(q, parts):
    lane = lax.broadcasted_iota(jnp.int32, (1, LANES), 1)
    outs = []
    for par in range(2):
        sel = (lane < NA_HEAD_DIM) if par == 0 else (lane >= NA_HEAD_DIM)
        qm = jnp.where(sel, q, jnp.zeros_like(q)) * ATT_SCALE
        scores = []
        for k, _, bias in parts:
            s = _dot_t(qm, k)
            if bias is not None:
                s = s + bias[0][bias[1] + par]
            scores.append(s)
        mx = scores[0].max(axis=-1, keepdims=True)
        for s in scores[1:]:
            mx = jnp.maximum(mx, s.max(axis=-1, keepdims=True))
        den = 0.0
        acc = 0.0
        for s, (_, v, _) in zip(scores, parts):
            pr = jnp.exp(s - mx)
            den = den + pr.sum(axis=-1, keepdims=True)
            acc = acc + _dot(pr.astype(BF16), v)
        outs.append(acc / den)
    return jnp.where(lane < NA_HEAD_DIM, outs[0], outs[1])


def _attn_ctx_kernel(q_ref, k_ref, v_ref, o_ref):
    for s in range(D_MODEL // LANES):
        sl = slice(s * LANES, (s + 1) * LANES)
        k = k_ref[:, sl].astype(BF16)
        v = v_ref[:, sl].astype(BF16)
        o_ref[:, sl] = _attend(q_ref[:, sl], [(k, v, None)]).astype(o_ref.dtype)


def _attn_ctx(q, k, v, batch):
    blk = pl.BlockSpec((PS, D_MODEL), lambda b: (b, 0))
    return pl.pallas_call(
        _attn_ctx_kernel,
        grid=(batch,),
        in_specs=[blk, blk, blk],
        out_specs=blk,
        out_shape=jax.ShapeDtypeStruct(q.shape, BF16),
        compiler_params=_cparams(("parallel",)),
        name="attn_ctx",
    )(q, k, v)


def _attn_dec_kernel(q_ref, k_ref, v_ref, kc_ref, vc_ref, bias_ref, o_ref):
    i = pl.program_id(0)
    row0 = jnp.clip(Q_ROWS * i - NA_WIN_ROWS // 2, 0, GRID_H - WIN_ROWS_BLK)
    win = pl.ds(pl.multiple_of(row0 * GRID_W, GRID_W), WIN_KEYS)
    for s in range(DEC_SLABS):
        sl = slice(s * LANES, (s + 1) * LANES)
        kc = kc_ref[:, sl].astype(BF16)
        vc = vc_ref[:, sl].astype(BF16)
        o = _attend(q_ref[:, sl], [(k_ref[win, sl], v_ref[win, sl], (bias_ref, 2 * s)),
                                   (kc, vc, None)])
        o_ref[:, sl] = o.astype(o_ref.dtype)


def _qblk_window(i):
    row0 = min(max(Q_ROWS * i - NA_WIN_ROWS // 2, 0), GRID_H - WIN_ROWS_BLK)
    out = []
    for a in range(Q_ROWS):
        r = Q_ROWS * i + a
        rs = min(max(r - NA_WIN_ROWS // 2, 0), GRID_H - NA_WIN_ROWS)
        out.append((r, [rs <= row0 + w < rs + NA_WIN_ROWS for w in range(WIN_ROWS_BLK)]))
    return row0, out


BIAS_CLASSES = (0, 1, N_QBLK - 1)


def _bias_kernel(rpb_ref, o_ref):
    c = lax.broadcasted_iota(jnp.int32, (GRID_W, LANES), 0)
    l = lax.broadcasted_iota(jnp.int32, (GRID_W, LANES), 1)
    cs = jnp.clip(c - NA_WIN_COLS // 2, 0, GRID_W - NA_WIN_COLS)
    in_cols = (l >= cs) & (l < cs + NA_WIN_COLS)
    neg = jnp.full((GRID_W, LANES), NEG_BIG, F32)
    lo_half, hi_half = [], []
    for dr in range(2 * NA_WIN_ROWS - 1):
        row = jnp.broadcast_to(rpb_ref[dr:dr + 1, :], (GRID_W, LANES))
        t = pltpu.roll(row, LANES - (NA_WIN_COLS - 1), 1, stride=1, stride_axis=0)
        t = jnp.where(in_cols, t, NEG_BIG)
        lo_half.append(t)
        hi_half.append(pltpu.roll(t, GRID_W, 1))
    for cls, i in enumerate(BIAS_CLASSES):
        row0, qrows = _qblk_window(i)
        for a, (r, valid) in enumerate(qrows):
            for wp in range(WIN_ROWS_BLK // 2):
                halves = []
                for half, bank in enumerate((lo_half, hi_half)):
                    w = 2 * wp + half
                    halves.append(bank[row0 + w - r + NA_WIN_ROWS - 1] if valid[w] else neg)
                o_ref[cls, a * GRID_W:(a + 1) * GRID_W, wp * LANES:(wp + 1) * LANES] = (
                    jnp.where(l < GRID_W, halves[0], halves[1]))


def _attn_bias_table(rpb, j):
    nl, nh, ndr, ndc = rpb.shape
    rpb_p = jnp.pad(rpb, ((0, 0), (0, 0), (0, 16 - ndr), (0, LANES - ndc)))
    return pl.pallas_call(
        _bias_kernel,
        grid=(nh,),
        in_specs=[pl.BlockSpec((None, None, 16, LANES), lambda h: (j, h, 0, 0))],
        out_specs=pl.BlockSpec((len(BIAS_CLASSES), None, Q_BLK, WIN_KEYS), lambda h: (0, h, 0, 0)),
        out_shape=jax.ShapeDtypeStruct((len(BIAS_CLASSES), nh, Q_BLK, WIN_KEYS), F32),
        compiler_params=_cparams(("parallel",)),
        name="attn_bias",
    )(rpb_p)


def _attn_dec(q, k, v, kc, vc, j, bias, batch):
    t = GRID_H * GRID_W
    assert DEC_SLABS * LANES == D_MODEL
    qblk = pl.BlockSpec((Q_BLK, D_MODEL), lambda i, b: (b * N_QBLK + i, 0))
    kvblk = pl.BlockSpec((None, t, D_MODEL), lambda i, b: (b, 0, 0))
    cblk = pl.BlockSpec((None, None, PS, D_MODEL), lambda i, b: (b, j, 0, 0))

    def bias_idx(i, b):
        cls = (i > 0).astype(jnp.int32) + (i == N_QBLK - 1).astype(jnp.int32)
        return (cls, 0, 0, 0)

    return pl.pallas_call(
        _attn_dec_kernel,
        grid=(N_QBLK, batch),
        in_specs=[qblk, kvblk, kvblk, cblk, cblk,
                  pl.BlockSpec((None, NA_HEADS, Q_BLK, WIN_KEYS), bias_idx,
                               pipeline_mode=pl.Buffered(1))],
        out_specs=qblk,
        out_shape=jax.ShapeDtypeStruct(q.shape, BF16),
        compiler_params=_cparams(("arbitrary", "arbitrary")),
        name="attn_dec",
    )(q, k.reshape(batch, t, D_MODEL), v.reshape(batch, t, D_MODEL), kc, vc, bias)


def _pffn_kernel(*refs, conf, final, mod_row):
    refs = list(refs)
    x_ref, y_ref, mod_ref, wp_ref = refs[:4]
    pos = 4
    bp_ref = lng_ref = lnb_ref = fin_ref = None
    if conf:
        bp_ref, lng_ref, lnb_ref = refs[pos:pos + 3]
        pos += 3
    w1_ref, w2_ref = refs[pos:pos + 2]
    pos += 2
    if final:
        fin_ref = refs[pos]
        pos += 1
    o_ref, h2_ref, acc_ref = refs[pos:pos + 3]
    f = pl.program_id(1)

    @pl.when(f == 0)
    def _():
        _, _, g1, sh2, sc2, _ = _mod_parts(mod_ref, mod_row)
        if conf:
            z = y_ref[...]
            mu = jnp.mean(z, axis=-1, keepdims=True)
            zc = z - mu
            var = jnp.mean(zc * zc, axis=-1, keepdims=True)
            zn = zc * lax.rsqrt(var + EPS) * lng_ref[...] + lnb_ref[...]
            y = (zn * _sigmoid(zn)).astype(BF16)
        else:
            y = y_ref[...]
        proj = _dot(y, wp_ref[...].astype(BF16))
        if bp_ref is not None:
            proj = proj + bp_ref[...]
        x1 = x_ref[...] + g1 * proj
        o_ref[...] = x1
        h2_ref[...] = (_rms(x1) * (1.0 + sc2) + sh2).astype(BF16)
        acc_ref[...] = jnp.zeros_like(acc_ref)

    u = _dot(h2_ref[...], w1_ref[...])
    u = jnp.square(jnp.maximum(u, 0.0)).astype(BF16)
    acc_ref[...] += _dot(u, w2_ref[...])

    @pl.when(f == pl.num_programs(1) - 1)
    def _():
        g2 = _mod_parts(mod_ref, mod_row)[5]
        out = o_ref[...] + g2 * acc_ref[...]
        if final:
            out = _rms(out) * fin_ref[...]
        o_ref[...] = out


def _pffn(x2, y, mods, layer, mod_row, w_proj, j, w1, w2, b_proj=None, ln_g=None, ln_b=None,
          final_g=None):
    rows = x2.shape[0]
    conf = ln_g is not None
    final = final_g is not None
    row_spec = pl.BlockSpec((TM, D_MODEL), lambda i, f: (i, 0))
    vec_spec = pl.BlockSpec((None, 1, D_MODEL), lambda i, f: (j, 0, 0))
    in_specs = [row_spec, row_spec,
                pl.BlockSpec((None, 8, 6 * D_MODEL), lambda i, f: (layer, 0, 0)),
                pl.BlockSpec((None, D_MODEL, D_MODEL), lambda i, f: (j, 0, 0),
                             pipeline_mode=pl.Buffered(1))]
    args = [x2, y, mods, w_proj]
    if conf:
        in_specs += [vec_spec, vec_spec, vec_spec]
        args += [v.reshape(v.shape[0], 1, D_MODEL) for v in (b_proj, ln_g, ln_b)]
    in_specs += [pl.BlockSpec((None, D_MODEL, FK), lambda i, f: (layer, 0, f)),
                 pl.BlockSpec((None, FK, D_MODEL), lambda i, f: (layer, f, 0))]
    args += [w1, w2]
    if final:
        in_specs.append(pl.BlockSpec((1, D_MODEL), lambda i, f: (0, 0)))
        args.append(final_g.reshape(1, D_MODEL))
    return pl.pallas_call(
        functools.partial(_pffn_kernel, conf=conf, final=final, mod_row=mod_row),
        grid=(rows // TM, D_FF // FK),
        in_specs=in_specs,
        out_specs=row_spec,
        out_shape=jax.ShapeDtypeStruct((rows, D_MODEL), F32),
        scratch_shapes=[pltpu.VMEM((TM, D_MODEL), BF16), pltpu.VMEM((TM, D_MODEL), F32)],
        compiler_params=_cparams(("parallel", "arbitrary")),
        name="pffn" + ("_conf" if conf else "") + ("_final" if final else ""),
    )(*args)


def _trunk(x, mods, is_ctx, p, state_lru, cache_k, cache_v, bias_tabs):
    bsz, t, d = x.shape
    rows = bsz * t
    x2 = x.reshape(rows, d)
    depth = mods.shape[0]
    row_mm = _make_mod_row(is_ctx, t, TM_MM)
    row_ffn = _make_mod_row(is_ctx, t, TM)
    row_seq = _make_mod_row(is_ctx, t, GROUP_ROWS)
    states, ks, vs = [], [], []
    for i in range(depth):
        kind, j = i % 3, i // 3
        fin = p["final_g"] if i == depth - 1 else None
        ffn = functools.partial(_pffn, x2, mods=mods, layer=i, mod_row=row_ffn, j=j,
                                w1=p["w_ff1"], w2=p["w_ff2"], final_g=fin)
        if kind == 0:
            y, st = _lru_seq(x2, mods, i, row_seq, p, j, None if is_ctx else state_lru)
            if is_ctx:
                states.append(st)
            x2 = ffn(y=y, w_proj=p["lru_w_out"])
        elif kind == 1:
            zc = _conv_seq(x2, mods, i, row_seq, p, j, chunked=not is_ctx)
            x2 = ffn(y=zc, w_proj=p["conf_w_pw2"], b_proj=p["conf_b_pw2"],
                     ln_g=p["conf_ln_g"], ln_b=p["conf_ln_b"])
        else:
            if is_ctx:
                q, k, v, k_cache, v_cache = _qkv(x2, mods, i, row_mm, p["na_w_qkv"], j, True)
                o = _attn_ctx(q, k, v, bsz)
                ks.append(k_cache.reshape(bsz, t, NA_HEADS, NA_HEAD_DIM))
                vs.append(v_cache.reshape(bsz, t, NA_HEADS, NA_HEAD_DIM))
            else:
                q, k, v = _qkv(x2, mods, i, row_mm, p["na_w_qkv"], j, False)
                kc = cache_k.reshape(cache_k.shape[:3] + (d,))
                vc = cache_v.reshape(cache_v.shape[:3] + (d,))
                o = _attn_dec(q, k, v, kc, vc, j, bias_tabs[j], bsz)
            x2 = ffn(y=o, w_proj=p["na_w_o"])
    return x2.reshape(bsz, t, d), states, ks, vs


def kernel(x_prompt, x_sample, state_lru, cache_k, cache_v, c, c_ctx, w_mod, b_mod, w_ff1, w_ff2, lru_w_in, lru_conv_w, lru_conv_b, lru_w_a, lru_b_a, lru_w_x, lru_b_x, lru_lambda, lru_w_out, conf_w_pw1, conf_b_pw1, conf_dw_w, conf_dw_b, conf_ln_g, conf_ln_b, conf_w_pw2, conf_b_pw2, na_w_qkv, na_w_o, na_rpb, final_g):
    p = dict(w_ff1=_to_bf16(w_ff1), w_ff2=_to_bf16(w_ff2), lru_w_in=lru_w_in, lru_conv_w=lru_conv_w,
             lru_conv_b=lru_conv_b, lru_w_a=lru_w_a, lru_b_a=lru_b_a, lru_w_x=lru_w_x,
             lru_b_x=lru_b_x, lru_lambda=lru_lambda, lru_w_out=lru_w_out,
             conf_w_pw1=conf_w_pw1, conf_b_pw1=conf_b_pw1, conf_dw_w=conf_dw_w,
             conf_dw_b=conf_dw_b, conf_ln_g=conf_ln_g, conf_ln_b=conf_ln_b,
             conf_w_pw2=conf_w_pw2, conf_b_pw2=conf_b_pw2, na_w_qkv=na_w_qkv, na_w_o=na_w_o,
             final_g=final_g)
    dec_b = c.shape[0]
    assert 1 + dec_b <= 8
    cond8 = jnp.concatenate([c_ctx[None, :], c, jnp.zeros((8 - 1 - dec_b, D_MODEL), F32)], axis=0)
    mods = _adaln(cond8, w_mod, b_mod)
    bias_tabs = [_attn_bias_table(na_rpb, j) for j in range(na_rpb.shape[0])]

    y_prompt, states, ks, vs = _trunk(x_prompt, mods, True, p, None, None, None, None)
    y_sample, _, _, _ = _trunk(x_sample, mods, False, p, state_lru, cache_k, cache_v, bias_tabs)

    new_state = jnp.stack([jnp.transpose(s, (1, 0, 2)) for s in states], axis=1)
    new_k = jnp.stack(ks, axis=1)
    new_v = jnp.stack(vs, axis=1)
    return (y_prompt, y_sample, new_state, new_k, new_v)
```

```python
import functools
import math

import jax
import jax.numpy as jnp
from jax import lax
from jax.experimental import pallas as pl
from jax.experimental.pallas import tpu as pltpu

F32 = jnp.float32
BF16 = jnp.bfloat16

D_MODEL = 1024
D_FF = 4 * D_MODEL
PS = 256
CB = 256
NB = D_MODEL // CB
SLOTS = 8
GROUP_ROWS = SLOTS * PS
GRID_W = 64
GRID_H = 32
NA_HEADS = 16
NA_HEAD_DIM = 64
NA_WIN_ROWS = 8
NA_WIN_COLS = 16
ATT_SCALE = NA_HEAD_DIM ** -0.5
assert math.frexp(ATT_SCALE)[0] == 0.5, "the attention kernels scale bf16 queries exactly"
LRU_C = 8.0
LRU_CONV_W = 4
CONF_CONV_W = 31
EPS = 1e-6
NEG_BIG = -1e30

LANES = 128
TM = 1024
TM_MM = 512
ROW_BLK = 256
FK = 1024
CAST_TILE = 1024
MOD_TILE = 1536
Q_ROWS = 4
Q_BLK = Q_ROWS * GRID_W
N_QBLK = GRID_H // Q_ROWS
WIN_ROWS_BLK = 12
WIN_KEYS = WIN_ROWS_BLK * GRID_W
DEC_SLABS = 8
VMEM_LIMIT = 56 * 1024 * 1024


def _cparams(sem):
    return pltpu.CompilerParams(dimension_semantics=sem, vmem_limit_bytes=VMEM_LIMIT)


def _dot(a, b):
    return jnp.dot(a, b, preferred_element_type=F32)


def _dot_t(a, b):
    return lax.dot_general(a, b, (((1,), (1,)), ((), ())), preferred_element_type=F32)


def _rms(x):
    return x * lax.rsqrt(jnp.mean(x * x, axis=-1, keepdims=True) + EPS)


def _sigmoid(x):
    return 0.5 * jnp.tanh(0.5 * x) + 0.5


def _mod_parts(mod_ref, mod_row):
    m = mod_ref[pl.ds(mod_row(pl.program_id(0)), 1), :]
    return [m[:, k * D_MODEL:(k + 1) * D_MODEL] for k in range(6)]


def _make_mod_row(is_ctx, seq_len, tile_rows):
    if is_ctx:
        return lambda i: 0
    return lambda i: 1 + (i * tile_rows) // seq_len


def _adaln_kernel(c_ref, w_ref, b_ref, o_ref):
    c = c_ref[...]
    s = (c * jax.nn.sigmoid(c)).astype(BF16)
    o_ref[...] = _dot(s, w_ref[...].astype(BF16)) + b_ref[...]


def _adaln(cond8, w_mod, b_mod):
    depth = w_mod.shape[0]
    n_out = w_mod.shape[2]
    return pl.pallas_call(
        _adaln_kernel,
        grid=(depth, n_out // MOD_TILE),
        in_specs=[
            pl.BlockSpec((8, D_MODEL), lambda l, n: (0, 0)),
            pl.BlockSpec((None, D_MODEL, MOD_TILE), lambda l, n: (l, 0, n)),
            pl.BlockSpec((None, 1, MOD_TILE), lambda l, n: (l, 0, n)),
        ],
        out_specs=pl.BlockSpec((None, 8, MOD_TILE), lambda l, n: (l, 0, n)),
        out_shape=jax.ShapeDtypeStruct((depth, 8, n_out), F32),
        compiler_params=_cparams(("parallel", "parallel")),
        name="adaln",
    )(cond8, w_mod, b_mod.reshape(depth, 1, n_out))


def _cast_kernel(w_ref, o_ref):
    o_ref[...] = w_ref[...].astype(o_ref.dtype)


def _to_bf16(w):
    nl, a, b = w.shape
    blk = pl.BlockSpec((None, CAST_TILE, CAST_TILE), lambda l, i, k: (l, i, k))
    return pl.pallas_call(
        _cast_kernel,
        grid=(nl, a // CAST_TILE, b // CAST_TILE),
        in_specs=[blk],
        out_specs=blk,
        out_shape=jax.ShapeDtypeStruct(w.shape, BF16),
        compiler_params=_cparams(("parallel", "parallel", "parallel")),
        name="cast_bf16",
    )(w)


def _modulated(x_ref, mod_ref, mod_row):
    sh1, sc1 = _mod_parts(mod_ref, mod_row)[:2]
    return (_rms(x_ref[...]) * (1.0 + sc1) + sh1).astype(BF16)


def _qkv_kernel(x_ref, mod_ref, w_ref, *refs, mod_row, with_cache):
    qkv_refs, wbf_ref = refs[:3], refs[-1]

    @pl.when(pl.program_id(0) == 0)
    def _():
        wbf_ref[...] = w_ref[...].astype(BF16)

    sh1, sc1 = _mod_parts(mod_ref, mod_row)[:2]
    for rb in range(TM_MM // ROW_BLK):
        rows = slice(rb * ROW_BLK, (rb + 1) * ROW_BLK)
        h = (_rms(x_ref[rows, :]) * (1.0 + sc1) + sh1).astype(BF16)
        for g, o_ref in enumerate(qkv_refs):
            o = _dot(h, wbf_ref[:, g * D_MODEL:(g + 1) * D_MODEL])
            o_ref[rows, :] = o.astype(o_ref.dtype)
            if with_cache and g > 0:
                refs[2 + g][rows] = o.reshape(ROW_BLK, NA_HEADS, NA_HEAD_DIM)


def _qkv(x2, mods, layer, mod_row, w, j, with_cache):
    rows = x2.shape[0]
    n_cols = w.shape[2]
    row_spec = pl.BlockSpec((TM_MM, D_MODEL), lambda i: (i, 0))
    out_specs = [row_spec] * 3
    out_shape = [jax.ShapeDtypeStruct((rows, D_MODEL), BF16)] * 3
    if with_cache:
        out_specs += [pl.BlockSpec((TM_MM, NA_HEADS, NA_HEAD_DIM), lambda i: (i, 0, 0))] * 2
        out_shape += [jax.ShapeDtypeStruct((rows, NA_HEADS, NA_HEAD_DIM), F32)] * 2
    return pl.pallas_call(
        functools.partial(_qkv_kernel, mod_row=mod_row, with_cache=with_cache),
        grid=(rows // TM_MM,),
        in_specs=[row_spec,
                  pl.BlockSpec((None, 8, 6 * D_MODEL), lambda i: (layer, 0, 0)),
                  pl.BlockSpec((None, D_MODEL, n_cols), lambda i: (j, 0, 0),
                               pipeline_mode=pl.Buffered(1))],
        out_specs=out_specs,
        out_shape=out_shape,
        scratch_shapes=[pltpu.VMEM((D_MODEL, n_cols), BF16)],
        compiler_params=_cparams(("arbitrary",)),
        name="modmm_qkv",
    )(x2, mods, w)


def _to_time_major(x2):
    return jnp.swapaxes(x2.reshape(SLOTS, PS, CB), 0, 1)


def _from_time_major(x3):
    return jnp.swapaxes(x3, 0, 1).reshape(GROUP_ROWS, CB)


def _slot_iota():
    return lax.broadcasted_iota(jnp.int32, (SLOTS, CB), 0)


def _from_prev_slot(tile):
    return jnp.where(_slot_iota() == 0, 0.0, pltpu.roll(tile, 1, 0))


def _from_next_slot(tile):
    return jnp.where(_slot_iota() == SLOTS - 1, 0.0, pltpu.roll(tile, SLOTS - 1, 0))


def _fill_padded(pad_ref, x, lo, hi, chunked):
    pad_ref[lo:lo + PS] = _to_time_major(x)
    for r in range(lo):
        if chunked:
            pad_ref[r] = _from_prev_slot(pad_ref[PS + r])
        else:
            pad_ref[r] = jnp.zeros((SLOTS, CB), F32)
    for r in range(hi):
        if chunked:
            pad_ref[lo + PS + r] = _from_next_slot(pad_ref[lo + r])
        else:
            pad_ref[lo + PS + r] = jnp.zeros((SLOTS, CB), F32)


def _group_spec():
    return pl.BlockSpec((GROUP_ROWS, CB), lambda g, n: (g, n))


def _chan_spec(lead, j, col0=0):
    return pl.BlockSpec((None, lead, CB), lambda g, n: (j, 0, col0 + n))


def _seq_in_specs(layer):
    return [pl.BlockSpec((GROUP_ROWS, D_MODEL), lambda g, n: (g, 0)),
            pl.BlockSpec((None, 8, 6 * D_MODEL), lambda g, n: (layer, 0, 0))]


def _seq_modulated(x_ref, mod_ref, h_ref, mod_row):
    @pl.when(pl.program_id(1) == 0)
    def _():
        h_ref[...] = _modulated(x_ref, mod_ref, mod_row)


LRU_TC = 32


def _softplus(x):
    return jnp.maximum(x, 0.0) + jnp.log1p(jnp.exp(-jnp.abs(x)))


def _lru_seq_kernel(*refs, chunked, mod_row):
    if chunked:
        (x_ref, mod_ref, wg_ref, wr_ref, cw_ref, cb_ref, wa_ref, ba_ref, wx_ref, bx_ref, lam_ref,
         h0_ref, y_ref, h_ref, gate_ref, pad_ref, af_ref, bf_ref, ab_ref, bb_ref, wbf_ref) = refs
    else:
        (x_ref, mod_ref, wg_ref, wr_ref, cw_ref, cb_ref, wa_ref, ba_ref, wx_ref, bx_ref, lam_ref,
         y_ref, st_ref, h_ref, gate_ref, pad_ref, af_ref, bf_ref, ab_ref, bb_ref, wbf_ref) = refs
    lo = (LRU_CONV_W - 1) // 2
    hi = LRU_CONV_W - 1 - lo
    _seq_modulated(x_ref, mod_ref, h_ref, mod_row)
    h = h_ref[...]
    gate_ref[...] = jax.nn.gelu(_dot(h, wg_ref[...].astype(BF16)), approximate=True)
    _fill_padded(pad_ref, _dot(h, wr_ref[...].astype(BF16)), lo, hi, chunked)

    a_refs = (af_ref, ab_ref)
    b_refs = (bf_ref, bb_ref)
    hc = [-0.5 * LRU_C * _softplus(-lam_ref[d:d + 1, :]) for d in range(2)]
    for d in range(2):
        wbf_ref[2 * d] = (0.5 * wa_ref[d]).astype(BF16)
        wbf_ref[2 * d + 1] = (0.5 * wx_ref[d]).astype(BF16)
    hba = [0.5 * ba_ref[d:d + 1, :] for d in range(2)]
    hbx = [0.5 * bx_ref[d:d + 1, :] for d in range(2)]

    def gates(ci, carry):
        t0 = pl.multiple_of(ci * LRU_TC, LRU_TC)
        xf = cb_ref[...] + cw_ref[0:1, :] * pad_ref[pl.ds(t0, LRU_TC)]
        for k in range(1, LRU_CONV_W):
            xf = xf + cw_ref[k:k + 1, :] * pad_ref[pl.ds(t0 + k, LRU_TC)]
        x2 = xf.reshape(LRU_TC * SLOTS, CB)
        xb = x2.astype(BF16)
        hx = 0.5 * x2
        for d in range(2):
            tr = jnp.tanh(_dot(xb, wbf_ref[2 * d]) + hba[d])
            ti = jnp.tanh(_dot(xb, wbf_ref[2 * d + 1]) + hbx[d])
            log_a = hc[d] * tr + hc[d]
            a = jnp.exp(log_a)
            one_m_a2 = -jnp.tanh(log_a) * (a * a + 1.0)
            root = jnp.where(one_m_a2 > 0.0, one_m_a2 * lax.rsqrt(one_m_a2), 0.0)
            bx = root * (hx * ti + hx)
            a_refs[d][pl.ds(t0, LRU_TC)] = a.reshape(LRU_TC, SLOTS, CB)
            b_refs[d][pl.ds(t0, LRU_TC)] = bx.reshape(LRU_TC, SLOTS, CB)
        return carry

    lax.fori_loop(0, PS // LRU_TC, gates, 0)

    zero = jnp.zeros((SLOTS, CB), F32)
    one = jnp.ones((SLOTS, CB), F32)

    def two_steps(a_ref, b_ref, t0, t1, h, p):
        a0, a1 = a_ref[t0], a_ref[t1]
        b0, b1 = b_ref[t0], b_ref[t1]
        a01 = a1 * a0
        b_ref[t0] = a0 * h + b0
        h = a01 * h + (a1 * b0 + b1)
        b_ref[t1] = h
        if chunked:
            a_ref[t0] = a0 * p
            p = a01 * p
            a_ref[t1] = p
        return h, p

    def scan(i, carry):
        hf, hb, pf, pb = carry
        t = 2 * i
        hf, pf = two_steps(af_ref, bf_ref, t, t + 1, hf, pf)
        hb, pb = two_steps(ab_ref, bb_ref, PS - 1 - t, PS - 2 - t, hb, pb)
        return hf, hb, pf, pb

    lax.fori_loop(0, PS // 2, scan, (zero, zero, one, one), unroll=2)

    if chunked:
        slot = _slot_iota()
        h0f = jnp.broadcast_to(h0_ref[0:1, :], (SLOTS, CB))
        h0b = jnp.broadcast_to(h0_ref[1:2, :], (SLOTS, CB))
        end_f, prod_f = bf_ref[PS - 1], af_ref[PS - 1]
        end_b, prod_b = bb_ref[0], ab_ref[0]
        in_f = jnp.where(slot == 0, h0f, 0.0)
        in_b = jnp.where(slot == SLOTS - 1, h0b, 0.0)
        for _ in range(SLOTS - 1):
            in_f = jnp.where(slot == 0, h0f, pltpu.roll(end_f + prod_f * in_f, 1, 0))
            in_b = jnp.where(slot == SLOTS - 1, h0b,
                             pltpu.roll(end_b + prod_b * in_b, SLOTS - 1, 0))
    else:
        st_ref[0] = bf_ref[PS - 1]
        st_ref[1] = bb_ref[0]

    def combine(ci, carry):
        sl = pl.ds(pl.multiple_of(ci * LRU_TC, LRU_TC), LRU_TC)
        hs = bf_ref[sl] + bb_ref[sl]
        if chunked:
            hs = hs + af_ref[sl] * in_f + ab_ref[sl] * in_b
        bf_ref[sl] = hs
        return carry

    lax.fori_loop(0, PS // LRU_TC, combine, 0)
    y_ref[...] = (_from_time_major(bf_ref[...]) * gate_ref[...]).astype(y_ref.dtype)


def _lru_seq(x2, mods, layer, mod_row, p, j, state_lru=None):
    chunked = state_lru is not None
    rows = x2.shape[0]
    groups = rows // GROUP_ROWS
    n_layers = p["lru_conv_b"].shape[0]
    seq = _group_spec()
    wblk = pl.BlockSpec((None, 2, None, CB, CB), lambda g, n: (j, 0, n, 0, 0))
    in_specs = _seq_in_specs(layer) + [
        _chan_spec(D_MODEL, j), _chan_spec(D_MODEL, j, NB),
        _chan_spec(LRU_CONV_W, j), _chan_spec(1, j),
        wblk, _chan_spec(2, j), wblk, _chan_spec(2, j), _chan_spec(2, j)]
    args = [x2, mods] + [p["lru_w_in"]] * 2 + [
            p["lru_conv_w"], p["lru_conv_b"].reshape(n_layers, 1, D_MODEL),
            p["lru_w_a"], p["lru_b_a"], p["lru_w_x"], p["lru_b_x"], p["lru_lambda"]]
    y_shape = jax.ShapeDtypeStruct((rows, D_MODEL), BF16)
    if chunked:
        in_specs.append(pl.BlockSpec((None, None, 2, CB), lambda g, n: (g, j, 0, n)))
        args.append(state_lru)
        out_specs = [seq]
        out_shape = [y_shape]
    else:
        out_specs = [seq, pl.BlockSpec((2, SLOTS, CB), lambda g, n: (0, g, n))]
        out_shape = [y_shape, jax.ShapeDtypeStruct((2, groups * SLOTS, D_MODEL), F32)]
    tile = (PS, SLOTS, CB)
    outs = pl.pallas_call(
        functools.partial(_lru_seq_kernel, chunked=chunked, mod_row=mod_row),
        grid=(groups, NB),
        in_specs=in_specs,
        out_specs=out_specs,
        out_shape=out_shape,
        scratch_shapes=[pltpu.VMEM((GROUP_ROWS, D_MODEL), BF16), pltpu.VMEM((GROUP_ROWS, CB), F32),
                        pltpu.VMEM((PS + LRU_CONV_W - 1, SLOTS, CB), F32)]
        + [pltpu.VMEM(tile, F32) for _ in range(4)] + [pltpu.VMEM((4, CB, CB), BF16)],
        compiler_params=_cparams(("parallel", "arbitrary")),
        name="lru_seq_dec" if chunked else "lru_seq_ctx",
    )(*args)
    return (outs[0], None) if chunked else (outs[0], outs[1])


CONV_TC = 16


def _conv_seq_kernel(x_ref, mod_ref, wv_ref, wg_ref, bv_ref, bg_ref, w_ref, b_ref, o_ref,
                     h_ref, pad_ref, out_ref, *, chunked, mod_row):
    lo = (CONF_CONV_W - 1) // 2
    hi = CONF_CONV_W - 1 - lo
    _seq_modulated(x_ref, mod_ref, h_ref, mod_row)
    h = h_ref[...]
    val = _dot(h, wv_ref[...].astype(BF16)) + bv_ref[...]
    gate = _dot(h, wg_ref[...].astype(BF16)) + bg_ref[...]
    _fill_padded(pad_ref, val * _sigmoid(gate), lo, hi, chunked)

    def chunk(ci, carry):
        t0 = pl.multiple_of(ci * CONV_TC, CONV_TC)
        acc = b_ref[...] + w_ref[0:1, :] * pad_ref[pl.ds(t0, CONV_TC)]
        for k in range(1, CONF_CONV_W):
            acc = acc + w_ref[k:k + 1, :] * pad_ref[pl.ds(t0 + k, CONV_TC)]
        out_ref[pl.ds(t0, CONV_TC)] = acc
        return carry

    lax.fori_loop(0, PS // CONV_TC, chunk, 0)
    o_ref[...] = _from_time_major(out_ref[...])


def _conv_seq(x2, mods, layer, mod_row, p, j, chunked):
    rows = x2.shape[0]
    n_layers = p["conf_dw_b"].shape[0]
    w_pw1 = p["conf_w_pw1"]
    b_pw1 = p["conf_b_pw1"].reshape(n_layers, 1, 2 * D_MODEL)
    return pl.pallas_call(
        functools.partial(_conv_seq_kernel, chunked=chunked, mod_row=mod_row),
        grid=(rows // GROUP_ROWS, NB),
        in_specs=_seq_in_specs(layer) + [
            _chan_spec(D_MODEL, j), _chan_spec(D_MODEL, j, NB),
            _chan_spec(1, j), _chan_spec(1, j, NB),
            _chan_spec(CONF_CONV_W, j), _chan_spec(1, j)],
        out_specs=_group_spec(),
        out_shape=jax.ShapeDtypeStruct((rows, D_MODEL), F32),
        scratch_shapes=[pltpu.VMEM((GROUP_ROWS, D_MODEL), BF16),
                        pltpu.VMEM((PS + CONF_CONV_W - 1, SLOTS, CB), F32),
                        pltpu.VMEM((PS, SLOTS, CB), F32)],
        compiler_params=_cparams(("parallel", "arbitrary")),
        name="conv_seq_dec" if chunked else "conv_seq_ctx",
    )(x2, mods, w_pw1, w_pw1, b_pw1, b_pw1, p["conf_dw_w"],
      p["conf_dw_b"].reshape(n_layers, 1, D_MODEL))


def _attend(q, parts):
    lane = lax.broadcasted_iota(jnp.int32, (1, LANES), 1)
    outs = []
    for par in range(2):
        sel = (lane < NA_HEAD_DIM) if par == 0 else (lane >= NA_HEAD_DIM)
        qm = jnp.where(sel, q, jnp.zeros_like(q)) * ATT_SCALE
        scores = []
        for k, _, bias in parts:
            s = _dot_t(qm, k)
            if bias is not None:
                s = s + bias[0][bias[1] + par]
            scores.append(s)
        mx = scores[0].max(axis=-1, keepdims=True)
        for s in scores[1:]:
            mx = jnp.maximum(mx, s.max(axis=-1, keepdims=True))
        den = 0.0
        acc = 0.0
        for s, (_, v, _) in zip(scores, parts):
            pr = jnp.exp(s - mx)
            den = den + pr.sum(axis=-1, keepdims=True)
            acc = acc + _dot(pr.astype(BF16), v)
        outs.append(acc / den)
    return jnp.where(lane < NA_HEAD_DIM, outs[0], outs[1])


def _attn_ctx_kernel(q_ref, k_ref, v_ref, o_ref):
    for s in range(D_MODEL // LANES):
        sl = slice(s * LANES, (s + 1) * LANES)
        k = k_ref[:, sl].astype(BF16)
        v = v_ref[:, sl].astype(BF16)
        o_ref[:, sl] = _attend(q_ref[:, sl], [(k, v, None)]).astype(o_ref.dtype)


def _attn_ctx(q, k, v, batch):
    blk = pl.BlockSpec((PS, D_MODEL), lambda b: (b, 0))
    return pl.pallas_call(
        _attn_ctx_kernel,
        grid=(batch,),
        in_specs=[blk, blk, blk],
        out_specs=blk,
        out_shape=jax.ShapeDtypeStruct(q.shape, BF16),
        compiler_params=_cparams(("parallel",)),
        name="attn_ctx",
    )(q, k, v)


def _attn_dec_kernel(q_ref, k_ref, v_ref, kc_ref, vc_ref, bias_ref, o_ref):
    i = pl.program_id(0)
    row0 = jnp.clip(Q_ROWS * i - NA_WIN_ROWS // 2, 0, GRID_H - WIN_ROWS_BLK)
    win = pl.ds(pl.multiple_of(row0 * GRID_W, GRID_W), WIN_KEYS)
    for s in range(DEC_SLABS):
        sl = slice(s * LANES, (s + 1) * LANES)
        kc = kc_ref[:, sl].astype(BF16)
        vc = vc_ref[:, sl].astype(BF16)
        o = _attend(q_ref[:, sl], [(k_ref[win, sl], v_ref[win, sl], (bias_ref, 2 * s)),
                                   (kc, vc, None)])
        o_ref[:, sl] = o.astype(o_ref.dtype)


def _qblk_window(i):
    row0 = min(max(Q_ROWS * i - NA_WIN_ROWS // 2, 0), GRID_H - WIN_ROWS_BLK)
    out = []
    for a in range(Q_ROWS):
        r = Q_ROWS * i + a
        rs = min(max(r - NA_WIN_ROWS // 2, 0), GRID_H - NA_WIN_ROWS)
        out.append((r, [rs <= row0 + w < rs + NA_WIN_ROWS for w in range(WIN_ROWS_BLK)]))
    return row0, out


BIAS_CLASSES = (0, 1, N_QBLK - 1)


def _bias_kernel(rpb_ref, o_ref):
    c = lax.broadcasted_iota(jnp.int32, (GRID_W, LANES), 0)
    l = lax.broadcasted_iota(jnp.int32, (GRID_W, LANES), 1)
    cs = jnp.clip(c - NA_WIN_COLS // 2, 0, GRID_W - NA_WIN_COLS)
    in_cols = (l >= cs) & (l < cs + NA_WIN_COLS)
    neg = jnp.full((GRID_W, LANES), NEG_BIG, F32)
    lo_half, hi_half = [], []
    for dr in range(2 * NA_WIN_ROWS - 1):
        row = jnp.broadcast_to(rpb_ref[dr:dr + 1, :], (GRID_W, LANES))
        t = pltpu.roll(row, LANES - (NA_WIN_COLS - 1), 1, stride=1, stride_axis=0)
        t = jnp.where(in_cols, t, NEG_BIG)
        lo_half.append(t)
        hi_half.append(pltpu.roll(t, GRID_W, 1))
    for cls, i in enumerate(BIAS_CLASSES):
        row0, qrows = _qblk_window(i)
        for a, (r, valid) in enumerate(qrows):
            for wp in range(WIN_ROWS_BLK // 2):
                halves = []
                for half, bank in enumerate((lo_half, hi_half)):
                    w = 2 * wp + half
                    halves.append(bank[row0 + w - r + NA_WIN_ROWS - 1] if valid[w] else neg)
                o_ref[cls, a * GRID_W:(a + 1) * GRID_W, wp * LANES:(wp + 1) * LANES] = (
                    jnp.where(l < GRID_W, halves[0], halves[1]))


def _attn_bias_table(rpb, j):
    nl, nh, ndr, ndc = rpb.shape
    rpb_p = jnp.pad(rpb, ((0, 0), (0, 0), (0, 16 - ndr), (0, LANES - ndc)))
    return pl.pallas_call(
        _bias_kernel,
        grid=(nh,),
        in_specs=[pl.BlockSpec((None, None, 16, LANES), lambda h: (j, h, 0, 0))],
        out_specs=pl.BlockSpec((len(BIAS_CLASSES), None, Q_BLK, WIN_KEYS), lambda h: (0, h, 0, 0)),
        out_shape=jax.ShapeDtypeStruct((len(BIAS_CLASSES), nh, Q_BLK, WIN_KEYS), F32),
        compiler_params=_cparams(("parallel",)),
        name="attn_bias",
    )(rpb_p)


def _attn_dec(q, k, v, kc, vc, j, bias, batch):
    t = GRID_H * GRID_W
    assert DEC_SLABS * LANES == D_MODEL
    qblk = pl.BlockSpec((Q_BLK, D_MODEL), lambda i, b: (b * N_QBLK + i, 0))
    kvblk = pl.BlockSpec((None, t, D_MODEL), lambda i, b: (b, 0, 0))
    cblk = pl.BlockSpec((None, None, PS, D_MODEL), lambda i, b: (b, j, 0, 0))

    def bias_idx(i, b):
        cls = (i > 0).astype(jnp.int32) + (i == N_QBLK - 1).astype(jnp.int32)
        return (cls, 0, 0, 0)

    return pl.pallas_call(
        _attn_dec_kernel,
        grid=(N_QBLK, batch),
        in_specs=[qblk, kvblk, kvblk, cblk, cblk,
                  pl.BlockSpec((None, NA_HEADS, Q_BLK, WIN_KEYS), bias_idx,
                               pipeline_mode=pl.Buffered(1))],
        out_specs=qblk,
        out_shape=jax.ShapeDtypeStruct(q.shape, BF16),
        compiler_params=_cparams(("arbitrary", "arbitrary")),
        name="attn_dec",
    )(q, k.reshape(batch, t, D_MODEL), v.reshape(batch, t, D_MODEL), kc, vc, bias)


def _pffn_kernel(*refs, conf, final, mod_row):
    refs = list(refs)
    x_ref, y_ref, mod_ref, wp_ref = refs[:4]
    pos = 4
    bp_ref = lng_ref = lnb_ref = fin_ref = None
    if conf:
        bp_ref, lng_ref, lnb_ref = refs[pos:pos + 3]
        pos += 3
    w1_ref, w2_ref = refs[pos:pos + 2]
    pos += 2
    if final:
        fin_ref = refs[pos]
        pos += 1
    o_ref, h2_ref, acc_ref, wpb_ref = refs[pos:pos + 4]
    f = pl.program_id(1)

    @pl.when(f == 0)
    def _():
        _, _, g1, sh2, sc2, _ = _mod_parts(mod_ref, mod_row)
        wpb_ref[...] = wp_ref[...].astype(BF16)
        for rb in range(TM // ROW_BLK):
            rows = slice(rb * ROW_BLK, (rb + 1) * ROW_BLK)
            if conf:
                z = y_ref[rows, :]
                mu = jnp.mean(z, axis=-1, keepdims=True)
                zc = z - mu
                var = jnp.mean(zc * zc, axis=-1, keepdims=True)
                zn = zc * lax.rsqrt(var + EPS) * lng_ref[...] + lnb_ref[...]
                y = (zn * _sigmoid(zn)).astype(BF16)
            else:
                y = y_ref[rows, :]
            proj = _dot(y, wpb_ref[...])
            if bp_ref is not None:
                proj = proj + bp_ref[...]
            x1 = x_ref[rows, :] + g1 * proj
            o_ref[rows, :] = x1
            h2_ref[rows, :] = (_rms(x1) * (1.0 + sc2) + sh2).astype(BF16)
        acc_ref[...] = jnp.zeros_like(acc_ref)

    u = _dot(h2_ref[...], w1_ref[...])
    u = jnp.square(jnp.maximum(u, 0.0)).astype(BF16)
    acc_ref[...] += _dot(u, w2_ref[...])

    @pl.when(f == pl.num_programs(1) - 1)
    def _():
        g2 = _mod_parts(mod_ref, mod_row)[5]
        out = o_ref[...] + g2 * acc_ref[...]
        if final:
            out = _rms(out) * fin_ref[...]
        o_ref[...] = out


def _pffn(x2, y, mods, layer, mod_row, w_proj, j, w1, w2, b_proj=None, ln_g=None, ln_b=None,
          final_g=None):
    rows = x2.shape[0]
    conf = ln_g is not None
    final = final_g is not None
    row_spec = pl.BlockSpec((TM, D_MODEL), lambda i, f: (i, 0))
    vec_spec = pl.BlockSpec((None, 1, D_MODEL), lambda i, f: (j, 0, 0))
    in_specs = [row_spec, row_spec,
                pl.BlockSpec((None, 8, 6 * D_MODEL), lambda i, f: (layer, 0, 0)),
                pl.BlockSpec((None, D_MODEL, D_MODEL), lambda i, f: (j, 0, 0),
                             pipeline_mode=pl.Buffered(1))]
    args = [x2, y, mods, w_proj]
    if conf:
        in_specs += [vec_spec, vec_spec, vec_spec]
        args += [v.reshape(v.shape[0], 1, D_MODEL) for v in (b_proj, ln_g, ln_b)]
    in_specs += [pl.BlockSpec((None, D_MODEL, FK), lambda i, f: (layer, 0, f)),
                 pl.BlockSpec((None, FK, D_MODEL), lambda i, f: (layer, f, 0))]
    args += [w1, w2]
    if final:
        in_specs.append(pl.BlockSpec((1, D_MODEL), lambda i, f: (0, 0)))
        args.append(final_g.reshape(1, D_MODEL))
    return pl.pallas_call(
        functools.partial(_pffn_kernel, conf=conf, final=final, mod_row=mod_row),
        grid=(rows // TM, D_FF // FK),
        in_specs=in_specs,
        out_specs=row_spec,
        out_shape=jax.ShapeDtypeStruct((rows, D_MODEL), F32),
        scratch_shapes=[pltpu.VMEM((TM, D_MODEL), BF16), pltpu.VMEM((TM, D_MODEL), F32),
                        pltpu.VMEM((D_MODEL, D_MODEL), BF16)],
        compiler_params=_cparams(("parallel", "arbitrary")),
        name="pffn" + ("_conf" if conf else "") + ("_final" if final else ""),
    )(*args)


def _trunk(x, mods, is_ctx, p, state_lru, cache_k, cache_v, bias_tabs):
    bsz, t, d = x.shape
    rows = bsz * t
    x2 = x.reshape(rows, d)
    depth = mods.shape[0]
    row_mm = _make_mod_row(is_ctx, t, TM_MM)
    row_ffn = _make_mod_row(is_ctx, t, TM)
    row_seq = _make_mod_row(is_ctx, t, GROUP_ROWS)
    states, ks, vs = [], [], []
    for i in range(depth):
        kind, j = i % 3, i // 3
        fin = p["final_g"] if i == depth - 1 else None
        ffn = functools.partial(_pffn, x2, mods=mods, layer=i, mod_row=row_ffn, j=j,
                                w1=p["w_ff1"], w2=p["w_ff2"], final_g=fin)
        if kind == 0:
            y, st = _lru_seq(x2, mods, i, row_seq, p, j, None if is_ctx else state_lru)
            if is_ctx:
                states.append(st)
            x2 = ffn(y=y, w_proj=p["lru_w_out"])
        elif kind == 1:
            zc = _conv_seq(x2, mods, i, row_seq, p, j, chunked=not is_ctx)
            x2 = ffn(y=zc, w_proj=p["conf_w_pw2"], b_proj=p["conf_b_pw2"],
                     ln_g=p["conf_ln_g"], ln_b=p["conf_ln_b"])
        else:
            if is_ctx:
                q, k, v, k_cache, v_cache = _qkv(x2, mods, i, row_mm, p["na_w_qkv"], j, True)
                o = _attn_ctx(q, k, v, bsz)
                ks.append(k_cache.reshape(bsz, t, NA_HEADS, NA_HEAD_DIM))
                vs.append(v_cache.reshape(bsz, t, NA_HEADS, NA_HEAD_DIM))
            else:
                q, k, v = _qkv(x2, mods, i, row_mm, p["na_w_qkv"], j, False)
                kc = cache_k.reshape(cache_k.shape[:3] + (d,))
                vc = cache_v.reshape(cache_v.shape[:3] + (d,))
                o = _attn_dec(q, k, v, kc, vc, j, bias_tabs[j], bsz)
            x2 = ffn(y=o, w_proj=p["na_w_o"])
    return x2.reshape(bsz, t, d), states, ks, vs


def kernel(x_prompt, x_sample, state_lru, cache_k, cache_v, c, c_ctx, w_mod, b_mod, w_ff1, w_ff2, lru_w_in, lru_conv_w, lru_conv_b, lru_w_a, lru_b_a, lru_w_x, lru_b_x, lru_lambda, lru_w_out, conf_w_pw1, conf_b_pw1, conf_dw_w, conf_dw_b, conf_ln_g, conf_ln_b, conf_w_pw2, conf_b_pw2, na_w_qkv, na_w_o, na_rpb, final_g):
    p = dict(w_ff1=_to_bf16(w_ff1), w_ff2=_to_bf16(w_ff2), lru_w_in=lru_w_in, lru_conv_w=lru_conv_w,
             lru_conv_b=lru_conv_b, lru_w_a=lru_w_a, lru_b_a=lru_b_a, lru_w_x=lru_w_x,
             lru_b_x=lru_b_x, lru_lambda=lru_lambda, lru_w_out=lru_w_out,
             conf_w_pw1=conf_w_pw1, conf_b_pw1=conf_b_pw1, conf_dw_w=conf_dw_w,
             conf_dw_b=conf_dw_b, conf_ln_g=conf_ln_g, conf_ln_b=conf_ln_b,
             conf_w_pw2=conf_w_pw2, conf_b_pw2=conf_b_pw2, na_w_qkv=na_w_qkv, na_w_o=na_w_o,
             final_g=final_g)
    dec_b = c.shape[0]
    assert 1 + dec_b <= 8
    cond8 = jnp.concatenate([c_ctx[None, :], c, jnp.zeros((8 - 1 - dec_b, D_MODEL), F32)], axis=0)
    mods = _adaln(cond8, w_mod, b_mod)
    bias_tabs = [_attn_bias_table(na_rpb, j) for j in range(na_rpb.shape[0])]

    y_prompt, states, ks, vs = _trunk(x_prompt, mods, True, p, None, None, None, None)
    y_sample, _, _, _ = _trunk(x_sample, mods, False, p, state_lru, cache_k, cache_v, bias_tabs)

    new_state = jnp.stack([jnp.transpose(s, (1, 0, 2)) for s in states], axis=1)
    new_k = jnp.stack(ks, axis=1)
    new_v = jnp.stack(vs, axis=1)
    return (y_prompt, y_sample, new_state, new_k, new_v)
```

```python
import functools
import math

import jax
import jax.numpy as jnp
from jax import lax
from jax.experimental import pallas as pl
from jax.experimental.pallas import tpu as pltpu

F32 = jnp.float32
BF16 = jnp.bfloat16

D_MODEL = 1024
D_FF = 4 * D_MODEL
PS = 256
CB = 256
NB = D_MODEL // CB
SLOTS = 8
GROUP_ROWS = SLOTS * PS
GRID_W = 64
GRID_H = 32
NA_HEADS = 16
NA_HEAD_DIM = 64
NA_WIN_ROWS = 8
NA_WIN_COLS = 16
ATT_SCALE = NA_HEAD_DIM ** -0.5
assert math.frexp(ATT_SCALE)[0] == 0.5, "the attention kernels scale bf16 queries exactly"
LRU_C = 8.0
LRU_CONV_W = 4
CONF_CONV_W = 31
EPS = 1e-6
NEG_BIG = -1e30

LANES = 128
SUBLANES = 8
MOD_ROWS = SUBLANES
TM = 1024
TM_MM = 512
ROW_BLK = 256
FK = 1024
CAST_TILE = 1024
MOD_TILE = 1536
Q_ROWS = 4
Q_BLK = Q_ROWS * GRID_W
N_QBLK = GRID_H // Q_ROWS
WIN_ROWS_BLK = 12
WIN_KEYS = WIN_ROWS_BLK * GRID_W
DEC_SLABS = 8
VMEM_LIMIT = 56 * 1024 * 1024


def _cparams(sem):
    return pltpu.CompilerParams(dimension_semantics=sem, vmem_limit_bytes=VMEM_LIMIT)


def _dot(a, b):
    return jnp.dot(a, b, preferred_element_type=F32)


def _dot_t(a, b):
    return lax.dot_general(a, b, (((1,), (1,)), ((), ())), preferred_element_type=F32)


def _rms(x):
    return x * lax.rsqrt(jnp.mean(x * x, axis=-1, keepdims=True) + EPS)


def _sigmoid(x):
    return 0.5 * jnp.tanh(0.5 * x) + 0.5


def _mod_parts(mod_ref, mod_row):
    m = mod_ref[pl.ds(mod_row(pl.program_id(0)), 1), :]
    return [m[:, k * D_MODEL:(k + 1) * D_MODEL] for k in range(6)]


def _make_mod_row(is_ctx, seq_len, tile_rows):
    if is_ctx:
        return lambda i: 0
    return lambda i: 1 + (i * tile_rows) // seq_len


def _adaln_kernel(c_ref, w_ref, b_ref, o_ref):
    c = c_ref[...]
    s = (c * jax.nn.sigmoid(c)).astype(BF16)
    o_ref[...] = _dot(s, w_ref[...].astype(BF16)) + b_ref[...]


def _adaln(cond8, w_mod, b_mod):
    depth = w_mod.shape[0]
    n_out = w_mod.shape[2]
    return pl.pallas_call(
        _adaln_kernel,
        grid=(depth, n_out // MOD_TILE),
        in_specs=[
            pl.BlockSpec((MOD_ROWS, D_MODEL), lambda l, n: (0, 0)),
            pl.BlockSpec((None, D_MODEL, MOD_TILE), lambda l, n: (l, 0, n)),
            pl.BlockSpec((None, 1, MOD_TILE), lambda l, n: (l, 0, n)),
        ],
        out_specs=pl.BlockSpec((None, MOD_ROWS, MOD_TILE), lambda l, n: (l, 0, n)),
        out_shape=jax.ShapeDtypeStruct((depth, MOD_ROWS, n_out), F32),
        compiler_params=_cparams(("parallel", "parallel")),
        name="adaln",
    )(cond8, w_mod, b_mod.reshape(depth, 1, n_out))


def _cast_kernel(w_ref, o_ref):
    o_ref[...] = w_ref[...].astype(o_ref.dtype)


def _to_bf16(w):
    nl, a, b = w.shape
    blk = pl.BlockSpec((None, CAST_TILE, CAST_TILE), lambda l, i, k: (l, i, k))
    return pl.pallas_call(
        _cast_kernel,
        grid=(nl, a // CAST_TILE, b // CAST_TILE),
        in_specs=[blk],
        out_specs=blk,
        out_shape=jax.ShapeDtypeStruct(w.shape, BF16),
        compiler_params=_cparams(("parallel", "parallel", "parallel")),
        name="cast_bf16",
    )(w)


def _modulated(x_ref, mod_ref, mod_row):
    sh1, sc1 = _mod_parts(mod_ref, mod_row)[:2]
    return (_rms(x_ref[...]) * (1.0 + sc1) + sh1).astype(BF16)


def _qkv_kernel(x_ref, mod_ref, w_ref, *refs, mod_row, with_cache):
    qkv_refs, wbf_ref = refs[:3], refs[-1]

    @pl.when(pl.program_id(0) == 0)
    def _():
        wbf_ref[...] = w_ref[...].astype(BF16)

    sh1, sc1 = _mod_parts(mod_ref, mod_row)[:2]
    for rb in range(TM_MM // ROW_BLK):
        rows = slice(rb * ROW_BLK, (rb + 1) * ROW_BLK)
        h = (_rms(x_ref[rows, :]) * (1.0 + sc1) + sh1).astype(BF16)
        for g, o_ref in enumerate(qkv_refs):
            o = _dot(h, wbf_ref[:, g * D_MODEL:(g + 1) * D_MODEL])
            o_ref[rows, :] = o.astype(o_ref.dtype)
            if with_cache and g > 0:
                refs[2 + g][rows] = o.reshape(ROW_BLK, NA_HEADS, NA_HEAD_DIM)


def _qkv(x2, mods, layer, mod_row, w, j, with_cache):
    rows = x2.shape[0]
    n_cols = w.shape[2]
    row_spec = pl.BlockSpec((TM_MM, D_MODEL), lambda i: (i, 0))
    out_specs = [row_spec] * 3
    out_shape = [jax.ShapeDtypeStruct((rows, D_MODEL), BF16)] * 3
    if with_cache:
        out_specs += [pl.BlockSpec((TM_MM, NA_HEADS, NA_HEAD_DIM), lambda i: (i, 0, 0))] * 2
        out_shape += [jax.ShapeDtypeStruct((rows, NA_HEADS, NA_HEAD_DIM), F32)] * 2
    return pl.pallas_call(
        functools.partial(_qkv_kernel, mod_row=mod_row, with_cache=with_cache),
        grid=(rows // TM_MM,),
        in_specs=[row_spec,
                  pl.BlockSpec((None, MOD_ROWS, 6 * D_MODEL), lambda i: (layer, 0, 0)),
                  pl.BlockSpec((None, D_MODEL, n_cols), lambda i: (j, 0, 0),
                               pipeline_mode=pl.Buffered(1))],
        out_specs=out_specs,
        out_shape=out_shape,
        scratch_shapes=[pltpu.VMEM((D_MODEL, n_cols), BF16)],
        compiler_params=_cparams(("arbitrary",)),
        name="modmm_qkv",
    )(x2, mods, w)


def _to_time_major(x2):
    return jnp.swapaxes(x2.reshape(SLOTS, PS, CB), 0, 1)


def _from_time_major(x3):
    return jnp.swapaxes(x3, 0, 1).reshape(GROUP_ROWS, CB)


def _slot_iota():
    return lax.broadcasted_iota(jnp.int32, (SLOTS, CB), 0)


def _from_prev_slot(tile):
    return jnp.where(_slot_iota() == 0, 0.0, pltpu.roll(tile, 1, 0))


def _from_next_slot(tile):
    return jnp.where(_slot_iota() == SLOTS - 1, 0.0, pltpu.roll(tile, SLOTS - 1, 0))


def _fill_padded(pad_ref, x, lo, hi, chunked):
    pad_ref[lo:lo + PS] = _to_time_major(x)
    for r in range(lo):
        if chunked:
            pad_ref[r] = _from_prev_slot(pad_ref[PS + r])
        else:
            pad_ref[r] = jnp.zeros((SLOTS, CB), F32)
    for r in range(hi):
        if chunked:
            pad_ref[lo + PS + r] = _from_next_slot(pad_ref[lo + r])
        else:
            pad_ref[lo + PS + r] = jnp.zeros((SLOTS, CB), F32)


def _group_spec():
    return pl.BlockSpec((GROUP_ROWS, CB), lambda g, n: (g, n))


def _chan_spec(lead, j, col0=0):
    return pl.BlockSpec((None, lead, CB), lambda g, n: (j, 0, col0 + n))


def _seq_in_specs(layer):
    return [pl.BlockSpec((GROUP_ROWS, D_MODEL), lambda g, n: (g, 0)),
            pl.BlockSpec((None, MOD_ROWS, 6 * D_MODEL), lambda g, n: (layer, 0, 0))]


def _seq_modulated(x_ref, mod_ref, h_ref, mod_row):
    @pl.when(pl.program_id(1) == 0)
    def _():
        h_ref[...] = _modulated(x_ref, mod_ref, mod_row)


LRU_TC = 128


def _softplus(x):
    return jnp.maximum(x, 0.0) + jnp.log1p(jnp.exp(-jnp.abs(x)))


def _lru_seq_kernel(*refs, chunked, mod_row):
    if chunked:
        (x_ref, mod_ref, wg_ref, wr_ref, cw_ref, cb_ref, wa_ref, ba_ref, wx_ref, bx_ref, lam_ref,
         h0_ref, y_ref, h_ref, gate_ref, pad_ref, af_ref, bf_ref, ab_ref, bb_ref, wbf_ref) = refs
    else:
        (x_ref, mod_ref, wg_ref, wr_ref, cw_ref, cb_ref, wa_ref, ba_ref, wx_ref, bx_ref, lam_ref,
         y_ref, st_ref, h_ref, gate_ref, pad_ref, af_ref, bf_ref, ab_ref, bb_ref, wbf_ref) = refs
    lo = (LRU_CONV_W - 1) // 2
    hi = LRU_CONV_W - 1 - lo
    _seq_modulated(x_ref, mod_ref, h_ref, mod_row)
    h = h_ref[...]
    gate_ref[...] = jax.nn.gelu(_dot(h, wg_ref[...].astype(BF16)), approximate=True)
    _fill_padded(pad_ref, _dot(h, wr_ref[...].astype(BF16)), lo, hi, chunked)

    a_refs = (af_ref, ab_ref)
    b_refs = (bf_ref, bb_ref)
    hc = [-0.5 * LRU_C * _softplus(-lam_ref[d:d + 1, :]) for d in range(2)]
    for d in range(2):
        wbf_ref[2 * d] = (0.5 * wa_ref[d]).astype(BF16)
        wbf_ref[2 * d + 1] = (0.5 * wx_ref[d]).astype(BF16)
    hba = [0.5 * ba_ref[d:d + 1, :] for d in range(2)]
    hbx = [0.5 * bx_ref[d:d + 1, :] for d in range(2)]

    def gates(ci, carry):
        t0 = pl.multiple_of(ci * LRU_TC, LRU_TC)
        xf = cb_ref[...] + cw_ref[0:1, :] * pad_ref[pl.ds(t0, LRU_TC)]
        for k in range(1, LRU_CONV_W):
            xf = xf + cw_ref[k:k + 1, :] * pad_ref[pl.ds(t0 + k, LRU_TC)]
        x2 = xf.reshape(LRU_TC * SLOTS, CB)
        xb = x2.astype(BF16)
        hx = 0.5 * x2
        for d in range(2):
            tr = jnp.tanh(_dot(xb, wbf_ref[2 * d]) + hba[d])
            ti = jnp.tanh(_dot(xb, wbf_ref[2 * d + 1]) + hbx[d])
            log_a = hc[d] * tr + hc[d]
            a = jnp.exp(log_a)
            one_m_a2 = -jnp.tanh(log_a) * (a * a + 1.0)
            root = jnp.where(one_m_a2 > 0.0, one_m_a2 * lax.rsqrt(one_m_a2), 0.0)
            bx = root * (hx * ti + hx)
            a_refs[d][pl.ds(t0, LRU_TC)] = a.reshape(LRU_TC, SLOTS, CB)
            b_refs[d][pl.ds(t0, LRU_TC)] = bx.reshape(LRU_TC, SLOTS, CB)
        return carry

    lax.fori_loop(0, PS // LRU_TC, gates, 0)

    zero = jnp.zeros((SLOTS, CB), F32)
    one = jnp.ones((SLOTS, CB), F32)

    def two_steps(a_ref, b_ref, t0, t1, h, p):
        a0, a1 = a_ref[t0], a_ref[t1]
        b0, b1 = b_ref[t0], b_ref[t1]
        a01 = a1 * a0
        b_ref[t0] = a0 * h + b0
        h = a01 * h + (a1 * b0 + b1)
        b_ref[t1] = h
        if chunked:
            a_ref[t0] = a0 * p
            p = a01 * p
            a_ref[t1] = p
        return h, p

    def scan(i, carry):
        hf, hb, pf, pb = carry
        t = 2 * i
        hf, pf = two_steps(af_ref, bf_ref, t, t + 1, hf, pf)
        hb, pb = two_steps(ab_ref, bb_ref, PS - 1 - t, PS - 2 - t, hb, pb)
        return hf, hb, pf, pb

    lax.fori_loop(0, PS // 2, scan, (zero, zero, one, one), unroll=2)

    if chunked:
        slot = _slot_iota()
        h0f = jnp.broadcast_to(h0_ref[0:1, :], (SLOTS, CB))
        h0b = jnp.broadcast_to(h0_ref[1:2, :], (SLOTS, CB))
        end_f, prod_f = bf_ref[PS - 1], af_ref[PS - 1]
        end_b, prod_b = bb_ref[0], ab_ref[0]
        in_f = jnp.where(slot == 0, h0f, 0.0)
        in_b = jnp.where(slot == SLOTS - 1, h0b, 0.0)
        for _ in range(SLOTS - 1):
            in_f = jnp.where(slot == 0, h0f, pltpu.roll(end_f + prod_f * in_f, 1, 0))
            in_b = jnp.where(slot == SLOTS - 1, h0b,
                             pltpu.roll(end_b + prod_b * in_b, SLOTS - 1, 0))
    else:
        st_ref[0] = bf_ref[PS - 1]
        st_ref[1] = bb_ref[0]

    def combine(ci, carry):
        sl = pl.ds(pl.multiple_of(ci * LRU_TC, LRU_TC), LRU_TC)
        hs = bf_ref[sl] + bb_ref[sl]
        if chunked:
            hs = hs + af_ref[sl] * in_f + ab_ref[sl] * in_b
        bf_ref[sl] = hs
        return carry

    lax.fori_loop(0, PS // LRU_TC, combine, 0)
    y_ref[...] = (_from_time_major(bf_ref[...]) * gate_ref[...]).astype(y_ref.dtype)


def _lru_seq(x2, mods, layer, mod_row, p, j, state_lru=None):
    chunked = state_lru is not None
    rows = x2.shape[0]
    groups = rows // GROUP_ROWS
    n_layers = p["lru_conv_b"].shape[0]
    seq = _group_spec()
    wblk = pl.BlockSpec((None, 2, None, CB, CB), lambda g, n: (j, 0, n, 0, 0))
    in_specs = _seq_in_specs(layer) + [
        _chan_spec(D_MODEL, j), _chan_spec(D_MODEL, j, NB),
        _chan_spec(LRU_CONV_W, j), _chan_spec(1, j),
        wblk, _chan_spec(2, j), wblk, _chan_spec(2, j), _chan_spec(2, j)]
    args = [x2, mods] + [p["lru_w_in"]] * 2 + [
            p["lru_conv_w"], p["lru_conv_b"].reshape(n_layers, 1, D_MODEL),
            p["lru_w_a"], p["lru_b_a"], p["lru_w_x"], p["lru_b_x"], p["lru_lambda"]]
    y_shape = jax.ShapeDtypeStruct((rows, D_MODEL), BF16)
    if chunked:
        in_specs.append(pl.BlockSpec((None, None, 2, CB), lambda g, n: (g, j, 0, n)))
        args.append(state_lru)
        out_specs = [seq]
        out_shape = [y_shape]
    else:
        out_specs = [seq, pl.BlockSpec((2, SLOTS, CB), lambda g, n: (0, g, n))]
        out_shape = [y_shape, jax.ShapeDtypeStruct((2, groups * SLOTS, D_MODEL), F32)]
    tile = (PS, SLOTS, CB)
    outs = pl.pallas_call(
        functools.partial(_lru_seq_kernel, chunked=chunked, mod_row=mod_row),
        grid=(groups, NB),
        in_specs=in_specs,
        out_specs=out_specs,
        out_shape=out_shape,
        scratch_shapes=[pltpu.VMEM((GROUP_ROWS, D_MODEL), BF16), pltpu.VMEM((GROUP_ROWS, CB), F32),
                        pltpu.VMEM((PS + LRU_CONV_W - 1, SLOTS, CB), F32)]
        + [pltpu.VMEM(tile, F32) for _ in range(4)] + [pltpu.VMEM((4, CB, CB), BF16)],
        compiler_params=_cparams(("parallel", "arbitrary")),
        name="lru_seq_dec" if chunked else "lru_seq_ctx",
    )(*args)
    return (outs[0], None) if chunked else (outs[0], outs[1])


CONV_TC = 128


def _conv_seq_kernel(x_ref, mod_ref, wv_ref, wg_ref, bv_ref, bg_ref, w_ref, b_ref, o_ref,
                     h_ref, pad_ref, out_ref, *, chunked, mod_row):
    lo = (CONF_CONV_W - 1) // 2
    hi = CONF_CONV_W - 1 - lo
    _seq_modulated(x_ref, mod_ref, h_ref, mod_row)
    h = h_ref[...]
    val = _dot(h, wv_ref[...].astype(BF16)) + bv_ref[...]
    gate = _dot(h, wg_ref[...].astype(BF16)) + bg_ref[...]
    _fill_padded(pad_ref, val * _sigmoid(gate), lo, hi, chunked)

    def chunk(ci, carry):
        t0 = pl.multiple_of(ci * CONV_TC, CONV_TC)
        acc = b_ref[...] + w_ref[0:1, :] * pad_ref[pl.ds(t0, CONV_TC)]
        for k in range(1, CONF_CONV_W):
            acc = acc + w_ref[k:k + 1, :] * pad_ref[pl.ds(t0 + k, CONV_TC)]
        out_ref[pl.ds(t0, CONV_TC)] = acc
        return carry

    lax.fori_loop(0, PS // CONV_TC, chunk, 0)
    o_ref[...] = _from_time_major(out_ref[...])


def _conv_seq(x2, mods, layer, mod_row, p, j, chunked):
    rows = x2.shape[0]
    n_layers = p["conf_dw_b"].shape[0]
    w_pw1 = p["conf_w_pw1"]
    b_pw1 = p["conf_b_pw1"].reshape(n_layers, 1, 2 * D_MODEL)
    return pl.pallas_call(
        functools.partial(_conv_seq_kernel, chunked=chunked, mod_row=mod_row),
        grid=(rows // GROUP_ROWS, NB),
        in_specs=_seq_in_specs(layer) + [
            _chan_spec(D_MODEL, j), _chan_spec(D_MODEL, j, NB),
            _chan_spec(1, j), _chan_spec(1, j, NB),
            _chan_spec(CONF_CONV_W, j), _chan_spec(1, j)],
        out_specs=_group_spec(),
        out_shape=jax.ShapeDtypeStruct((rows, D_MODEL), F32),
        scratch_shapes=[pltpu.VMEM((GROUP_ROWS, D_MODEL), BF16),
                        pltpu.VMEM((PS + CONF_CONV_W - 1, SLOTS, CB), F32),
                        pltpu.VMEM((PS, SLOTS, CB), F32)],
        compiler_params=_cparams(("parallel", "arbitrary")),
        name="conv_seq_dec" if chunked else "conv_seq_ctx",
    )(x2, mods, w_pw1, w_pw1, b_pw1, b_pw1, p["conf_dw_w"],
      p["conf_dw_b"].reshape(n_layers, 1, D_MODEL))


def _attend(q, parts):
    lane = lax.broadcasted_iota(jnp.int32, (1, LANES), 1)
    m = q.shape[0]
    zero = jnp.zeros_like(q)
    qm = jnp.concatenate([jnp.where(lane < NA_HEAD_DIM, q, zero),
                          jnp.where(lane >= NA_HEAD_DIM, q, zero)], axis=0) * ATT_SCALE
    scores = []
    for k, _, bias in parts:
        s = _dot_t(qm, k)
        if bias is not None:
            ref, head = bias
            s = s + jnp.concatenate([ref[head], ref[head + 1]], axis=0)
        scores.append(s)
    mx = scores[0].max(axis=-1, keepdims=True)
    for s in scores[1:]:
        mx = jnp.maximum(mx, s.max(axis=-1, keepdims=True))
    den = 0.0
    acc = 0.0
    for s, (_, v, _) in zip(scores, parts):
        pr = jnp.exp(s - mx)
        den = den + pr.sum(axis=-1, keepdims=True)
        acc = acc + _dot(pr.astype(BF16), v)
    o = acc / den
    return jnp.where(lane < NA_HEAD_DIM, o[:m], o[m:])


def _attn_ctx_kernel(q_ref, k_ref, v_ref, o_ref):
    for s in range(D_MODEL // LANES):
        sl = slice(s * LANES, (s + 1) * LANES)
        k = k_ref[:, sl].astype(BF16)
        v = v_ref[:, sl].astype(BF16)
        o_ref[:, sl] = _attend(q_ref[:, sl], [(k, v, None)]).astype(o_ref.dtype)


def _attn_ctx(q, k, v, batch):
    blk = pl.BlockSpec((PS, D_MODEL), lambda b: (b, 0))
    return pl.pallas_call(
        _attn_ctx_kernel,
        grid=(batch,),
        in_specs=[blk, blk, blk],
        out_specs=blk,
        out_shape=jax.ShapeDtypeStruct(q.shape, BF16),
        compiler_params=_cparams(("parallel",)),
        name="attn_ctx",
    )(q, k, v)


def _attn_dec_kernel(q_ref, k_ref, v_ref, kc_ref, vc_ref, bias_ref, o_ref):
    i = pl.program_id(0)
    row0 = jnp.clip(Q_ROWS * i - NA_WIN_ROWS // 2, 0, GRID_H - WIN_ROWS_BLK)
    win = pl.ds(pl.multiple_of(row0 * GRID_W, GRID_W), WIN_KEYS)
    for s in range(DEC_SLABS):
        sl = slice(s * LANES, (s + 1) * LANES)
        kc = kc_ref[:, sl].astype(BF16)
        vc = vc_ref[:, sl].astype(BF16)
        o = _attend(q_ref[:, sl], [(k_ref[win, sl], v_ref[win, sl], (bias_ref, 2 * s)),
                                   (kc, vc, None)])
        o_ref[:, sl] = o.astype(o_ref.dtype)


def _qblk_window(i):
    row0 = min(max(Q_ROWS * i - NA_WIN_ROWS // 2, 0), GRID_H - WIN_ROWS_BLK)
    out = []
    for a in range(Q_ROWS):
        r = Q_ROWS * i + a
        rs = min(max(r - NA_WIN_ROWS // 2, 0), GRID_H - NA_WIN_ROWS)
        out.append((r, [rs <= row0 + w < rs + NA_WIN_ROWS for w in range(WIN_ROWS_BLK)]))
    return row0, out


BIAS_CLASSES = (0, 1, N_QBLK - 1)


def _bias_kernel(rpb_ref, o_ref):
    c = lax.broadcasted_iota(jnp.int32, (GRID_W, LANES), 0)
    l = lax.broadcasted_iota(jnp.int32, (GRID_W, LANES), 1)
    cs = jnp.clip(c - NA_WIN_COLS // 2, 0, GRID_W - NA_WIN_COLS)
    in_cols = (l >= cs) & (l < cs + NA_WIN_COLS)
    neg = jnp.full((GRID_W, LANES), NEG_BIG, F32)
    lo_half, hi_half = [], []
    for dr in range(2 * NA_WIN_ROWS - 1):
        row = jnp.broadcast_to(rpb_ref[dr:dr + 1, :], (GRID_W, LANES))
        t = pltpu.roll(row, LANES - (NA_WIN_COLS - 1), 1, stride=1, stride_axis=0)
        t = jnp.where(in_cols, t, NEG_BIG)
        lo_half.append(t)
        hi_half.append(pltpu.roll(t, GRID_W, 1))
    for cls, i in enumerate(BIAS_CLASSES):
        row0, qrows = _qblk_window(i)
        for a, (r, valid) in enumerate(qrows):
            for wp in range(WIN_ROWS_BLK // 2):
                halves = []
                for half, bank in enumerate((lo_half, hi_half)):
                    w = 2 * wp + half
                    halves.append(bank[row0 + w - r + NA_WIN_ROWS - 1] if valid[w] else neg)
                o_ref[cls, a * GRID_W:(a + 1) * GRID_W, wp * LANES:(wp + 1) * LANES] = (
                    jnp.where(l < GRID_W, halves[0], halves[1]))


def _attn_bias_table(rpb, j):
    nl, nh, ndr, ndc = rpb.shape
    ndr_pad = -(-ndr // SUBLANES) * SUBLANES
    rpb_p = jnp.pad(rpb, ((0, 0), (0, 0), (0, ndr_pad - ndr), (0, LANES - ndc)))
    return pl.pallas_call(
        _bias_kernel,
        grid=(nh,),
        in_specs=[pl.BlockSpec((None, None, ndr_pad, LANES), lambda h: (j, h, 0, 0))],
        out_specs=pl.BlockSpec((len(BIAS_CLASSES), None, Q_BLK, WIN_KEYS), lambda h: (0, h, 0, 0)),
        out_shape=jax.ShapeDtypeStruct((len(BIAS_CLASSES), nh, Q_BLK, WIN_KEYS), F32),
        compiler_params=_cparams(("parallel",)),
        name="attn_bias",
    )(rpb_p)


def _attn_dec(q, k, v, kc, vc, j, bias, batch):
    t = GRID_H * GRID_W
    assert DEC_SLABS * LANES == D_MODEL
    qblk = pl.BlockSpec((Q_BLK, D_MODEL), lambda i, b: (b * N_QBLK + i, 0))
    kvblk = pl.BlockSpec((None, t, D_MODEL), lambda i, b: (b, 0, 0))
    cblk = pl.BlockSpec((None, None, PS, D_MODEL), lambda i, b: (b, j, 0, 0))

    def bias_idx(i, b):
        cls = (i > 0).astype(jnp.int32) + (i == N_QBLK - 1).astype(jnp.int32)
        return (cls, 0, 0, 0)

    return pl.pallas_call(
        _attn_dec_kernel,
        grid=(N_QBLK, batch),
        in_specs=[qblk, kvblk, kvblk, cblk, cblk,
                  pl.BlockSpec((None, NA_HEADS, Q_BLK, WIN_KEYS), bias_idx,
                               pipeline_mode=pl.Buffered(1))],
        out_specs=qblk,
        out_shape=jax.ShapeDtypeStruct(q.shape, BF16),
        compiler_params=_cparams(("arbitrary", "arbitrary")),
        name="attn_dec",
    )(q, k.reshape(batch, t, D_MODEL), v.reshape(batch, t, D_MODEL), kc, vc, bias)


def _pffn_kernel(*refs, conf, final, mod_row):
    refs = list(refs)
    x_ref, y_ref, mod_ref, wp_ref = refs[:4]
    pos = 4
    bp_ref = lng_ref = lnb_ref = fin_ref = None
    if conf:
        bp_ref, lng_ref, lnb_ref = refs[pos:pos + 3]
        pos += 3
    w1_ref, w2_ref = refs[pos:pos + 2]
    pos += 2
    if final:
        fin_ref = refs[pos]
        pos += 1
    o_ref, h2_ref, acc_ref, wpb_ref = refs[pos:pos + 4]
    f = pl.program_id(1)

    @pl.when(f == 0)
    def _():
        _, _, g1, sh2, sc2, _ = _mod_parts(mod_ref, mod_row)
        wpb_ref[...] = wp_ref[...].astype(BF16)
        for rb in range(TM // ROW_BLK):
            rows = slice(rb * ROW_BLK, (rb + 1) * ROW_BLK)
            if conf:
                z = y_ref[rows, :]
                mu = jnp.mean(z, axis=-1, keepdims=True)
                zc = z - mu
                var = jnp.mean(zc * zc, axis=-1, keepdims=True)
                zn = zc * lax.rsqrt(var + EPS) * lng_ref[...] + lnb_ref[...]
                y = (zn * _sigmoid(zn)).astype(BF16)
            else:
                y = y_ref[rows, :]
            proj = _dot(y, wpb_ref[...])
            if bp_ref is not None:
                proj = proj + bp_ref[...]
            x1 = x_ref[rows, :] + g1 * proj
            o_ref[rows, :] = x1
            h2_ref[rows, :] = (_rms(x1) * (1.0 + sc2) + sh2).astype(BF16)
        acc_ref[...] = jnp.zeros_like(acc_ref)

    u = _dot(h2_ref[...], w1_ref[...])
    u = jnp.square(jnp.maximum(u, 0.0)).astype(BF16)
    acc_ref[...] += _dot(u, w2_ref[...])

    @pl.when(f == pl.num_programs(1) - 1)
    def _():
        g2 = _mod_parts(mod_ref, mod_row)[5]
        out = o_ref[...] + g2 * acc_ref[...]
        if final:
            out = _rms(out) * fin_ref[...]
        o_ref[...] = out


def _pffn(x2, y, mods, layer, mod_row, w_proj, j, w1, w2, b_proj=None, ln_g=None, ln_b=None,
          final_g=None):
    rows = x2.shape[0]
    conf = ln_g is not None
    final = final_g is not None
    row_spec = pl.BlockSpec((TM, D_MODEL), lambda i, f: (i, 0))
    vec_spec = pl.BlockSpec((None, 1, D_MODEL), lambda i, f: (j, 0, 0))
    in_specs = [row_spec, row_spec,
                pl.BlockSpec((None, MOD_ROWS, 6 * D_MODEL), lambda i, f: (layer, 0, 0)),
                pl.BlockSpec((None, D_MODEL, D_MODEL), lambda i, f: (j, 0, 0),
                             pipeline_mode=pl.Buffered(1))]
    args = [x2, y, mods, w_proj]
    if conf:
        in_specs += [vec_spec, vec_spec, vec_spec]
        args += [v.reshape(v.shape[0], 1, D_MODEL) for v in (b_proj, ln_g, ln_b)]
    in_specs += [pl.BlockSpec((None, D_MODEL, FK), lambda i, f: (layer, 0, f)),
                 pl.BlockSpec((None, FK, D_MODEL), lambda i, f: (layer, f, 0))]
    args += [w1, w2]
    if final:
        in_specs.append(pl.BlockSpec((1, D_MODEL), lambda i, f: (0, 0)))
        args.append(final_g.reshape(1, D_MODEL))
    return pl.pallas_call(
        functools.partial(_pffn_kernel, conf=conf, final=final, mod_row=mod_row),
        grid=(rows // TM, D_FF // FK),
        in_specs=in_specs,
        out_specs=row_spec,
        out_shape=jax.ShapeDtypeStruct((rows, D_MODEL), F32),
        scratch_shapes=[pltpu.VMEM((TM, D_MODEL), BF16), pltpu.VMEM((TM, D_MODEL), F32),
                        pltpu.VMEM((D_MODEL, D_MODEL), BF16)],
        compiler_params=_cparams(("parallel", "arbitrary")),
        name="pffn" + ("_conf" if conf else "") + ("_final" if final else ""),
    )(*args)


def _trunk(x, mods, is_ctx, p, state_lru, cache_k, cache_v, bias_tabs):
    bsz, t, d = x.shape
    rows = bsz * t
    assert d == D_MODEL and rows % GROUP_ROWS == 0 and rows % TM == 0
    if is_ctx:
        assert t == PS
    else:
        assert t == GROUP_ROWS == GRID_H * GRID_W and cache_k.shape[2] == PS
    x2 = x.reshape(rows, d)
    depth = mods.shape[0]
    row_mm = _make_mod_row(is_ctx, t, TM_MM)
    row_ffn = _make_mod_row(is_ctx, t, TM)
    row_seq = _make_mod_row(is_ctx, t, GROUP_ROWS)
    states, ks, vs = [], [], []
    for i in range(depth):
        kind, j = i % 3, i // 3
        fin = p["final_g"] if i == depth - 1 else None
        ffn = functools.partial(_pffn, x2, mods=mods, layer=i, mod_row=row_ffn, j=j,
                                w1=p["w_ff1"], w2=p["w_ff2"], final_g=fin)
        if kind == 0:
            y, st = _lru_seq(x2, mods, i, row_seq, p, j, None if is_ctx else state_lru)
            if is_ctx:
                states.append(st)
            x2 = ffn(y=y, w_proj=p["lru_w_out"])
        elif kind == 1:
            zc = _conv_seq(x2, mods, i, row_seq, p, j, chunked=not is_ctx)
            x2 = ffn(y=zc, w_proj=p["conf_w_pw2"], b_proj=p["conf_b_pw2"],
                     ln_g=p["conf_ln_g"], ln_b=p["conf_ln_b"])
        else:
            if is_ctx:
                q, k, v, k_cache, v_cache = _qkv(x2, mods, i, row_mm, p["na_w_qkv"], j, True)
                o = _attn_ctx(q, k, v, bsz)
                ks.append(k_cache.reshape(bsz, t, NA_HEADS, NA_HEAD_DIM))
                vs.append(v_cache.reshape(bsz, t, NA_HEADS, NA_HEAD_DIM))
            else:
                q, k, v = _qkv(x2, mods, i, row_mm, p["na_w_qkv"], j, False)
                kc = cache_k.reshape(cache_k.shape[:3] + (d,))
                vc = cache_v.reshape(cache_v.shape[:3] + (d,))
                o = _attn_dec(q, k, v, kc, vc, j, bias_tabs[j], bsz)
            x2 = ffn(y=o, w_proj=p["na_w_o"])
    return x2.reshape(bsz, t, d), states, ks, vs


def kernel(x_prompt, x_sample, state_lru, cache_k, cache_v, c, c_ctx, w_mod, b_mod, w_ff1, w_ff2, lru_w_in, lru_conv_w, lru_conv_b, lru_w_a, lru_b_a, lru_w_x, lru_b_x, lru_lambda, lru_w_out, conf_w_pw1, conf_b_pw1, conf_dw_w, conf_dw_b, conf_ln_g, conf_ln_b, conf_w_pw2, conf_b_pw2, na_w_qkv, na_w_o, na_rpb, final_g):
    p = dict(w_ff1=_to_bf16(w_ff1), w_ff2=_to_bf16(w_ff2), lru_w_in=lru_w_in, lru_conv_w=lru_conv_w,
             lru_conv_b=lru_conv_b, lru_w_a=lru_w_a, lru_b_a=lru_b_a, lru_w_x=lru_w_x,
             lru_b_x=lru_b_x, lru_lambda=lru_lambda, lru_w_out=lru_w_out,
             conf_w_pw1=conf_w_pw1, conf_b_pw1=conf_b_pw1, conf_dw_w=conf_dw_w,
             conf_dw_b=conf_dw_b, conf_ln_g=conf_ln_g, conf_ln_b=conf_ln_b,
             conf_w_pw2=conf_w_pw2, conf_b_pw2=conf_b_pw2, na_w_qkv=na_w_qkv, na_w_o=na_w_o,
             final_g=final_g)
    dec_b = c.shape[0]
    assert 1 + dec_b <= MOD_ROWS
    cond8 = jnp.concatenate(
        [c_ctx[None, :], c, jnp.zeros((MOD_ROWS - 1 - dec_b, D_MODEL), F32)], axis=0)
    mods = _adaln(cond8, w_mod, b_mod)
    bias_tabs = [_attn_bias_table(na_rpb, j) for j in range(na_rpb.shape[0])]

    y_prompt, states, ks, vs = _trunk(x_prompt, mods, True, p, None, None, None, None)
    y_sample, _, _, _ = _trunk(x_sample, mods, False, p, state_lru, cache_k, cache_v, bias_tabs)

    new_state = jnp.stack([jnp.transpose(s, (1, 0, 2)) for s in states], axis=1)
    new_k = jnp.stack(ks, axis=1)
    new_v = jnp.stack(vs, axis=1)
    return (y_prompt, y_sample, new_state, new_k, new_v)
```

```python
import functools
import math

import jax
import jax.numpy as jnp
from jax import lax
from jax.experimental import pallas as pl
from jax.experimental.pallas import tpu as pltpu

F32 = jnp.float32
BF16 = jnp.bfloat16

D_MODEL = 1024
D_FF = 4 * D_MODEL
PS = 256
CB = 256
NB = D_MODEL // CB
SLOTS = 8
GROUP_ROWS = SLOTS * PS
GRID_W = 64
GRID_H = 32
NA_HEADS = 16
NA_HEAD_DIM = 64
NA_WIN_ROWS = 8
NA_WIN_COLS = 16
ATT_SCALE = NA_HEAD_DIM ** -0.5
assert math.frexp(ATT_SCALE)[0] == 0.5, "the attention kernels scale bf16 queries exactly"
LRU_C = 8.0
LRU_CONV_W = 4
LOG2_E = math.log2(math.e)
CONF_CONV_W = 31
EPS = 1e-6
NEG_BIG = -1e30

LANES = 128
SUBLANES = 8
MOD_ROWS = SUBLANES
TM = 1024
TM_MM = 512
ROW_BLK = 256
FK = 1024
CAST_TILE = 1024
MOD_TILE = 1536
Q_ROWS = 4
Q_BLK = Q_ROWS * GRID_W
N_QBLK = GRID_H // Q_ROWS
WIN_ROWS_BLK = 12
WIN_KEYS = WIN_ROWS_BLK * GRID_W
DEC_SLABS = 8
VMEM_LIMIT = 56 * 1024 * 1024


def _cparams(sem):
    return pltpu.CompilerParams(dimension_semantics=sem, vmem_limit_bytes=VMEM_LIMIT)


def _dot(a, b):
    return jnp.dot(a, b, preferred_element_type=F32)


def _dot_t(a, b):
    return lax.dot_general(a, b, (((1,), (1,)), ((), ())), preferred_element_type=F32)


def _rms(x):
    return x * lax.rsqrt(jnp.mean(x * x, axis=-1, keepdims=True) + EPS)


def _sigmoid(x):
    return 0.5 * jnp.tanh(0.5 * x) + 0.5


def _mod_parts(mod_ref, mod_row):
    m = mod_ref[pl.ds(mod_row(pl.program_id(0)), 1), :]
    return [m[:, k * D_MODEL:(k + 1) * D_MODEL] for k in range(6)]


def _make_mod_row(is_ctx, seq_len, tile_rows):
    if is_ctx:
        return lambda i: 0
    return lambda i: 1 + (i * tile_rows) // seq_len


def _adaln_kernel(c_ref, w_ref, b_ref, o_ref):
    c = c_ref[...]
    s = (c * jax.nn.sigmoid(c)).astype(BF16)
    o_ref[...] = _dot(s, w_ref[...].astype(BF16)) + b_ref[...]


def _adaln(cond8, w_mod, b_mod):
    depth = w_mod.shape[0]
    n_out = w_mod.shape[2]
    return pl.pallas_call(
        _adaln_kernel,
        grid=(depth, n_out // MOD_TILE),
        in_specs=[
            pl.BlockSpec((MOD_ROWS, D_MODEL), lambda l, n: (0, 0)),
            pl.BlockSpec((None, D_MODEL, MOD_TILE), lambda l, n: (l, 0, n)),
            pl.BlockSpec((None, 1, MOD_TILE), lambda l, n: (l, 0, n)),
        ],
        out_specs=pl.BlockSpec((None, MOD_ROWS, MOD_TILE), lambda l, n: (l, 0, n)),
        out_shape=jax.ShapeDtypeStruct((depth, MOD_ROWS, n_out), F32),
        compiler_params=_cparams(("parallel", "parallel")),
        name="adaln",
    )(cond8, w_mod, b_mod.reshape(depth, 1, n_out))


def _cast_kernel(w_ref, o_ref):
    o_ref[...] = w_ref[...].astype(o_ref.dtype)


def _to_bf16(w):
    nl, a, b = w.shape
    blk = pl.BlockSpec((None, CAST_TILE, CAST_TILE), lambda l, i, k: (l, i, k))
    return pl.pallas_call(
        _cast_kernel,
        grid=(nl, a // CAST_TILE, b // CAST_TILE),
        in_specs=[blk],
        out_specs=blk,
        out_shape=jax.ShapeDtypeStruct(w.shape, BF16),
        compiler_params=_cparams(("parallel", "parallel", "parallel")),
        name="cast_bf16",
    )(w)


def _modulated(x_ref, mod_ref, mod_row):
    sh1, sc1 = _mod_parts(mod_ref, mod_row)[:2]
    return (_rms(x_ref[...]) * (1.0 + sc1) + sh1).astype(BF16)


def _qkv_kernel(x_ref, mod_ref, w_ref, *refs, mod_row, with_cache):
    qkv_refs, wbf_ref = refs[:3], refs[-1]

    @pl.when(pl.program_id(0) == 0)
    def _():
        wbf_ref[...] = w_ref[...].astype(BF16)

    sh1, sc1 = _mod_parts(mod_ref, mod_row)[:2]
    for rb in range(TM_MM // ROW_BLK):
        rows = slice(rb * ROW_BLK, (rb + 1) * ROW_BLK)
        h = (_rms(x_ref[rows, :]) * (1.0 + sc1) + sh1).astype(BF16)
        for g, o_ref in enumerate(qkv_refs):
            o = _dot(h, wbf_ref[:, g * D_MODEL:(g + 1) * D_MODEL])
            o_ref[rows, :] = o.astype(o_ref.dtype)
            if with_cache and g > 0:
                refs[2 + g][rows] = o.reshape(ROW_BLK, NA_HEADS, NA_HEAD_DIM)


def _qkv(x2, mods, layer, mod_row, w, j, with_cache):
    rows = x2.shape[0]
    n_cols = w.shape[2]
    row_spec = pl.BlockSpec((TM_MM, D_MODEL), lambda i: (i, 0))
    out_specs = [row_spec] * 3
    out_shape = [jax.ShapeDtypeStruct((rows, D_MODEL), BF16)] * 3
    if with_cache:
        out_specs += [pl.BlockSpec((TM_MM, NA_HEADS, NA_HEAD_DIM), lambda i: (i, 0, 0))] * 2
        out_shape += [jax.ShapeDtypeStruct((rows, NA_HEADS, NA_HEAD_DIM), F32)] * 2
    return pl.pallas_call(
        functools.partial(_qkv_kernel, mod_row=mod_row, with_cache=with_cache),
        grid=(rows // TM_MM,),
        in_specs=[row_spec,
                  pl.BlockSpec((None, MOD_ROWS, 6 * D_MODEL), lambda i: (layer, 0, 0)),
                  pl.BlockSpec((None, D_MODEL, n_cols), lambda i: (j, 0, 0),
                               pipeline_mode=pl.Buffered(1))],
        out_specs=out_specs,
        out_shape=out_shape,
        scratch_shapes=[pltpu.VMEM((D_MODEL, n_cols), BF16)],
        compiler_params=_cparams(("arbitrary",)),
        name="modmm_qkv",
    )(x2, mods, w)


def _to_time_major(x2):
    return jnp.swapaxes(x2.reshape(SLOTS, PS, CB), 0, 1)


def _from_time_major(x3):
    return jnp.swapaxes(x3, 0, 1).reshape(GROUP_ROWS, CB)


def _slot_iota():
    return lax.broadcasted_iota(jnp.int32, (SLOTS, CB), 0)


def _from_prev_slot(tile):
    return jnp.where(_slot_iota() == 0, 0.0, pltpu.roll(tile, 1, 0))


def _from_next_slot(tile):
    return jnp.where(_slot_iota() == SLOTS - 1, 0.0, pltpu.roll(tile, SLOTS - 1, 0))


def _fill_padded(pad_ref, x, lo, hi, chunked):
    pad_ref[lo:lo + PS] = _to_time_major(x)
    for r in range(lo):
        if chunked:
            pad_ref[r] = _from_prev_slot(pad_ref[PS + r])
        else:
            pad_ref[r] = jnp.zeros((SLOTS, CB), F32)
    for r in range(hi):
        if chunked:
            pad_ref[lo + PS + r] = _from_next_slot(pad_ref[lo + r])
        else:
            pad_ref[lo + PS + r] = jnp.zeros((SLOTS, CB), F32)


def _group_spec():
    return pl.BlockSpec((GROUP_ROWS, CB), lambda g, n: (g, n))


def _chan_spec(lead, j, col0=0):
    return pl.BlockSpec((None, lead, CB), lambda g, n: (j, 0, col0 + n))


def _seq_in_specs(layer):
    return [pl.BlockSpec((GROUP_ROWS, D_MODEL), lambda g, n: (g, 0)),
            pl.BlockSpec((None, MOD_ROWS, 6 * D_MODEL), lambda g, n: (layer, 0, 0))]


def _seq_modulated(x_ref, mod_ref, h_ref, mod_row):
    @pl.when(pl.program_id(1) == 0)
    def _():
        h_ref[...] = _modulated(x_ref, mod_ref, mod_row)


LRU_TC = 128


def _softplus(x):
    return jnp.maximum(x, 0.0) + jnp.log1p(jnp.exp(-jnp.abs(x)))


def _lru_seq_kernel(*refs, chunked, mod_row):
    if chunked:
        (x_ref, mod_ref, wg_ref, wr_ref, cw_ref, cb_ref, wa_ref, ba_ref, wx_ref, bx_ref, lam_ref,
         h0_ref, y_ref, h_ref, gate_ref, pad_ref, af_ref, bf_ref, ab_ref, bb_ref, wbf_ref) = refs
    else:
        (x_ref, mod_ref, wg_ref, wr_ref, cw_ref, cb_ref, wa_ref, ba_ref, wx_ref, bx_ref, lam_ref,
         y_ref, st_ref, h_ref, gate_ref, pad_ref, af_ref, bf_ref, ab_ref, bb_ref, wbf_ref) = refs
    lo = (LRU_CONV_W - 1) // 2
    hi = LRU_CONV_W - 1 - lo
    _seq_modulated(x_ref, mod_ref, h_ref, mod_row)
    h = h_ref[...]
    gate_ref[...] = jax.nn.gelu(_dot(h, wg_ref[...].astype(BF16)), approximate=True)
    _fill_padded(pad_ref, _dot(h, wr_ref[...].astype(BF16)), lo, hi, chunked)

    a_refs = (af_ref, ab_ref)
    b_refs = (bf_ref, bb_ref)
    hc = [0.5 * LRU_C * _softplus(-lam_ref[d:d + 1, :]) for d in range(2)]
    for d in range(2):
        wbf_ref[2 * d] = (0.5 * wa_ref[d]).astype(BF16)
        wbf_ref[2 * d + 1] = (0.5 * wx_ref[d]).astype(BF16)
    hba = [0.5 * ba_ref[d:d + 1, :] for d in range(2)]
    hbx = [0.5 * bx_ref[d:d + 1, :] for d in range(2)]

    def gates(ci, carry):
        t0 = pl.multiple_of(ci * LRU_TC, LRU_TC)
        xf = cb_ref[...] + cw_ref[0:1, :] * pad_ref[pl.ds(t0, LRU_TC)]
        for k in range(1, LRU_CONV_W):
            xf = xf + cw_ref[k:k + 1, :] * pad_ref[pl.ds(t0 + k, LRU_TC)]
        x2 = xf.reshape(LRU_TC * SLOTS, CB)
        xb = x2.astype(BF16)
        hx = 0.5 * x2
        for d in range(2):
            tr = jnp.tanh(_dot(xb, wbf_ref[2 * d]) + hba[d])
            ti = jnp.tanh(_dot(xb, wbf_ref[2 * d + 1]) + hbx[d])
            pos = hc[d] * tr + hc[d]
            a = jnp.exp2(pos * (-LOG2_E))
            one_m_a2 = jnp.tanh(pos) * (a * a + 1.0)
            root = jnp.where(one_m_a2 > 0.0, one_m_a2 * lax.rsqrt(one_m_a2), 0.0)
            bx = root * (hx * ti + hx)
            a_refs[d][pl.ds(t0, LRU_TC)] = a.reshape(LRU_TC, SLOTS, CB)
            b_refs[d][pl.ds(t0, LRU_TC)] = bx.reshape(LRU_TC, SLOTS, CB)
        return carry

    lax.fori_loop(0, PS // LRU_TC, gates, 0)

    zero = jnp.zeros((SLOTS, CB), F32)
    one = jnp.ones((SLOTS, CB), F32)

    def two_steps(a_ref, b_ref, t0, t1, h, p):
        a0, a1 = a_ref[t0], a_ref[t1]
        b0, b1 = b_ref[t0], b_ref[t1]
        a01 = a1 * a0
        b_ref[t0] = a0 * h + b0
        h = a01 * h + (a1 * b0 + b1)
        b_ref[t1] = h
        if chunked:
            a_ref[t0] = a0 * p
            p = a01 * p
            a_ref[t1] = p
        return h, p

    def scan(i, carry):
        hf, hb, pf, pb = carry
        t = 2 * i
        hf, pf = two_steps(af_ref, bf_ref, t, t + 1, hf, pf)
        hb, pb = two_steps(ab_ref, bb_ref, PS - 1 - t, PS - 2 - t, hb, pb)
        return hf, hb, pf, pb

    lax.fori_loop(0, PS // 2, scan, (zero, zero, one, one), unroll=2)

    if chunked:
        slot = _slot_iota()
        h0f = jnp.broadcast_to(h0_ref[0:1, :], (SLOTS, CB))
        h0b = jnp.broadcast_to(h0_ref[1:2, :], (SLOTS, CB))
        end_f, prod_f = bf_ref[PS - 1], af_ref[PS - 1]
        end_b, prod_b = bb_ref[0], ab_ref[0]
        in_f = jnp.where(slot == 0, h0f, 0.0)
        in_b = jnp.where(slot == SLOTS - 1, h0b, 0.0)
        for _ in range(SLOTS - 1):
            in_f = jnp.where(slot == 0, h0f, pltpu.roll(end_f + prod_f * in_f, 1, 0))
            in_b = jnp.where(slot == SLOTS - 1, h0b,
                             pltpu.roll(end_b + prod_b * in_b, SLOTS - 1, 0))
    else:
        st_ref[0] = bf_ref[PS - 1]
        st_ref[1] = bb_ref[0]

    def combine(ci, carry):
        sl = pl.ds(pl.multiple_of(ci * LRU_TC, LRU_TC), LRU_TC)
        hs = bf_ref[sl] + bb_ref[sl]
        if chunked:
            hs = hs + af_ref[sl] * in_f + ab_ref[sl] * in_b
        bf_ref[sl] = hs
        return carry

    lax.fori_loop(0, PS // LRU_TC, combine, 0)
    y_ref[...] = (_from_time_major(bf_ref[...]) * gate_ref[...]).astype(y_ref.dtype)


def _lru_seq(x2, mods, layer, mod_row, p, j, state_lru=None):
    chunked = state_lru is not None
    rows = x2.shape[0]
    groups = rows // GROUP_ROWS
    n_layers = p["lru_conv_b"].shape[0]
    seq = _group_spec()
    wblk = pl.BlockSpec((None, 2, None, CB, CB), lambda g, n: (j, 0, n, 0, 0))
    in_specs = _seq_in_specs(layer) + [
        _chan_spec(D_MODEL, j), _chan_spec(D_MODEL, j, NB),
        _chan_spec(LRU_CONV_W, j), _chan_spec(1, j),
        wblk, _chan_spec(2, j), wblk, _chan_spec(2, j), _chan_spec(2, j)]
    args = [x2, mods] + [p["lru_w_in"]] * 2 + [
            p["lru_conv_w"], p["lru_conv_b"].reshape(n_layers, 1, D_MODEL),
            p["lru_w_a"], p["lru_b_a"], p["lru_w_x"], p["lru_b_x"], p["lru_lambda"]]
    y_shape = jax.ShapeDtypeStruct((rows, D_MODEL), BF16)
    if chunked:
        in_specs.append(pl.BlockSpec((None, None, 2, CB), lambda g, n: (g, j, 0, n)))
        args.append(state_lru)
        out_specs = [seq]
        out_shape = [y_shape]
    else:
        out_specs = [seq, pl.BlockSpec((2, SLOTS, CB), lambda g, n: (0, g, n))]
        out_shape = [y_shape, jax.ShapeDtypeStruct((2, groups * SLOTS, D_MODEL), F32)]
    tile = (PS, SLOTS, CB)
    outs = pl.pallas_call(
        functools.partial(_lru_seq_kernel, chunked=chunked, mod_row=mod_row),
        grid=(groups, NB),
        in_specs=in_specs,
        out_specs=out_specs,
        out_shape=out_shape,
        scratch_shapes=[pltpu.VMEM((GROUP_ROWS, D_MODEL), BF16), pltpu.VMEM((GROUP_ROWS, CB), F32),
                        pltpu.VMEM((PS + LRU_CONV_W - 1, SLOTS, CB), F32)]
        + [pltpu.VMEM(tile, F32) for _ in range(4)] + [pltpu.VMEM((4, CB, CB), BF16)],
        compiler_params=_cparams(("parallel", "arbitrary")),
        name="lru_seq_dec" if chunked else "lru_seq_ctx",
    )(*args)
    return (outs[0], None) if chunked else (outs[0], outs[1])


CONV_TC = 128


def _conv_seq_kernel(x_ref, mod_ref, wv_ref, wg_ref, bv_ref, bg_ref, w_ref, b_ref, o_ref,
                     h_ref, pad_ref, out_ref, *, chunked, mod_row):
    lo = (CONF_CONV_W - 1) // 2
    hi = CONF_CONV_W - 1 - lo
    _seq_modulated(x_ref, mod_ref, h_ref, mod_row)
    h = h_ref[...]
    val = _dot(h, wv_ref[...].astype(BF16)) + bv_ref[...]
    gate = _dot(h, wg_ref[...].astype(BF16)) + bg_ref[...]
    _fill_padded(pad_ref, val * _sigmoid(gate), lo, hi, chunked)

    def chunk(ci, carry):
        t0 = pl.multiple_of(ci * CONV_TC, CONV_TC)
        acc = b_ref[...] + w_ref[0:1, :] * pad_ref[pl.ds(t0, CONV_TC)]
        for k in range(1, CONF_CONV_W):
            acc = acc + w_ref[k:k + 1, :] * pad_ref[pl.ds(t0 + k, CONV_TC)]
        out_ref[pl.ds(t0, CONV_TC)] = acc
        return carry

    lax.fori_loop(0, PS // CONV_TC, chunk, 0)
    o_ref[...] = _from_time_major(out_ref[...])


def _conv_seq(x2, mods, layer, mod_row, p, j, chunked):
    rows = x2.shape[0]
    n_layers = p["conf_dw_b"].shape[0]
    w_pw1 = p["conf_w_pw1"]
    b_pw1 = p["conf_b_pw1"].reshape(n_layers, 1, 2 * D_MODEL)
    return pl.pallas_call(
        functools.partial(_conv_seq_kernel, chunked=chunked, mod_row=mod_row),
        grid=(rows // GROUP_ROWS, NB),
        in_specs=_seq_in_specs(layer) + [
            _chan_spec(D_MODEL, j), _chan_spec(D_MODEL, j, NB),
            _chan_spec(1, j), _chan_spec(1, j, NB),
            _chan_spec(CONF_CONV_W, j), _chan_spec(1, j)],
        out_specs=_group_spec(),
        out_shape=jax.ShapeDtypeStruct((rows, D_MODEL), F32),
        scratch_shapes=[pltpu.VMEM((GROUP_ROWS, D_MODEL), BF16),
                        pltpu.VMEM((PS + CONF_CONV_W - 1, SLOTS, CB), F32),
                        pltpu.VMEM((PS, SLOTS, CB), F32)],
        compiler_params=_cparams(("parallel", "arbitrary")),
        name="conv_seq_dec" if chunked else "conv_seq_ctx",
    )(x2, mods, w_pw1, w_pw1, b_pw1, b_pw1, p["conf_dw_w"],
      p["conf_dw_b"].reshape(n_layers, 1, D_MODEL))


def _attend(q, parts):
    lane = lax.broadcasted_iota(jnp.int32, (1, LANES), 1)
    m = q.shape[0]
    zero = jnp.zeros_like(q)
    qm = jnp.concatenate([jnp.where(lane < NA_HEAD_DIM, q, zero),
                          jnp.where(lane >= NA_HEAD_DIM, q, zero)], axis=0) * ATT_SCALE
    scores = []
    for k, _, bias in parts:
        s = _dot_t(qm, k)
        if bias is not None:
            ref, head = bias
            s = s + jnp.concatenate([ref[head], ref[head + 1]], axis=0)
        scores.append(s)
    mx = scores[0].max(axis=-1, keepdims=True)
    for s in scores[1:]:
        mx = jnp.maximum(mx, s.max(axis=-1, keepdims=True))
    den = 0.0
    acc = 0.0
    for s, (_, v, _) in zip(scores, parts):
        pr = jnp.exp(s - mx)
        den = den + pr.sum(axis=-1, keepdims=True)
        acc = acc + _dot(pr.astype(BF16), v)
    o = acc / den
    return jnp.where(lane < NA_HEAD_DIM, o[:m], o[m:])


def _attn_ctx_kernel(q_ref, k_ref, v_ref, o_ref):
    for s in range(D_MODEL // LANES):
        sl = slice(s * LANES, (s + 1) * LANES)
        k = k_ref[:, sl].astype(BF16)
        v = v_ref[:, sl].astype(BF16)
        o_ref[:, sl] = _attend(q_ref[:, sl], [(k, v, None)]).astype(o_ref.dtype)


def _attn_ctx(q, k, v, batch):
    blk = pl.BlockSpec((PS, D_MODEL), lambda b: (b, 0))
    return pl.pallas_call(
        _attn_ctx_kernel,
        grid=(batch,),
        in_specs=[blk, blk, blk],
        out_specs=blk,
        out_shape=jax.ShapeDtypeStruct(q.shape, BF16),
        compiler_params=_cparams(("parallel",)),
        name="attn_ctx",
    )(q, k, v)


def _attn_dec_kernel(q_ref, k_ref, v_ref, kc_ref, vc_ref, bias_ref, o_ref):
    i = pl.program_id(0)
    row0 = jnp.clip(Q_ROWS * i - NA_WIN_ROWS // 2, 0, GRID_H - WIN_ROWS_BLK)
    win = pl.ds(pl.multiple_of(row0 * GRID_W, GRID_W), WIN_KEYS)
    for s in range(DEC_SLABS):
        sl = slice(s * LANES, (s + 1) * LANES)
        kc = kc_ref[:, sl].astype(BF16)
        vc = vc_ref[:, sl].astype(BF16)
        o = _attend(q_ref[:, sl], [(k_ref[win, sl], v_ref[win, sl], (bias_ref, 2 * s)),
                                   (kc, vc, None)])
        o_ref[:, sl] = o.astype(o_ref.dtype)


def _qblk_window(i):
    row0 = min(max(Q_ROWS * i - NA_WIN_ROWS // 2, 0), GRID_H - WIN_ROWS_BLK)
    out = []
    for a in range(Q_ROWS):
        r = Q_ROWS * i + a
        rs = min(max(r - NA_WIN_ROWS // 2, 0), GRID_H - NA_WIN_ROWS)
        out.append((r, [rs <= row0 + w < rs + NA_WIN_ROWS for w in range(WIN_ROWS_BLK)]))
    return row0, out


BIAS_CLASSES = (0, 1, N_QBLK - 1)


def _bias_kernel(rpb_ref, o_ref):
    c = lax.broadcasted_iota(jnp.int32, (GRID_W, LANES), 0)
    l = lax.broadcasted_iota(jnp.int32, (GRID_W, LANES), 1)
    cs = jnp.clip(c - NA_WIN_COLS // 2, 0, GRID_W - NA_WIN_COLS)
    in_cols = (l >= cs) & (l < cs + NA_WIN_COLS)
    neg = jnp.full((GRID_W, LANES), NEG_BIG, F32)
    lo_half, hi_half = [], []
    for dr in range(2 * NA_WIN_ROWS - 1):
        row = jnp.broadcast_to(rpb_ref[dr:dr + 1, :], (GRID_W, LANES))
        t = pltpu.roll(row, LANES - (NA_WIN_COLS - 1), 1, stride=1, stride_axis=0)
        t = jnp.where(in_cols, t, NEG_BIG)
        lo_half.append(t)
        hi_half.append(pltpu.roll(t, GRID_W, 1))
    for cls, i in enumerate(BIAS_CLASSES):
        row0, qrows = _qblk_window(i)
        for a, (r, valid) in enumerate(qrows):
            for wp in range(WIN_ROWS_BLK // 2):
                halves = []
                for half, bank in enumerate((lo_half, hi_half)):
                    w = 2 * wp + half
                    halves.append(bank[row0 + w - r + NA_WIN_ROWS - 1] if valid[w] else neg)
                o_ref[cls, a * GRID_W:(a + 1) * GRID_W, wp * LANES:(wp + 1) * LANES] = (
                    jnp.where(l < GRID_W, halves[0], halves[1]))


def _attn_bias_table(rpb, j):
    nl, nh, ndr, ndc = rpb.shape
    ndr_pad = -(-ndr // SUBLANES) * SUBLANES
    rpb_p = jnp.pad(rpb, ((0, 0), (0, 0), (0, ndr_pad - ndr), (0, LANES - ndc)))
    return pl.pallas_call(
        _bias_kernel,
        grid=(nh,),
        in_specs=[pl.BlockSpec((None, None, ndr_pad, LANES), lambda h: (j, h, 0, 0))],
        out_specs=pl.BlockSpec((len(BIAS_CLASSES), None, Q_BLK, WIN_KEYS), lambda h: (0, h, 0, 0)),
        out_shape=jax.ShapeDtypeStruct((len(BIAS_CLASSES), nh, Q_BLK, WIN_KEYS), F32),
        compiler_params=_cparams(("parallel",)),
        name="attn_bias",
    )(rpb_p)


def _attn_dec(q, k, v, kc, vc, j, bias, batch):
    t = GRID_H * GRID_W
    assert DEC_SLABS * LANES == D_MODEL
    qblk = pl.BlockSpec((Q_BLK, D_MODEL), lambda i, b: (b * N_QBLK + i, 0))
    kvblk = pl.BlockSpec((None, t, D_MODEL), lambda i, b: (b, 0, 0))
    cblk = pl.BlockSpec((None, None, PS, D_MODEL), lambda i, b: (b, j, 0, 0))

    def bias_idx(i, b):
        cls = (i > 0).astype(jnp.int32) + (i == N_QBLK - 1).astype(jnp.int32)
        return (cls, 0, 0, 0)

    return pl.pallas_call(
        _attn_dec_kernel,
        grid=(N_QBLK, batch),
        in_specs=[qblk, kvblk, kvblk, cblk, cblk,
                  pl.BlockSpec((None, NA_HEADS, Q_BLK, WIN_KEYS), bias_idx,
                               pipeline_mode=pl.Buffered(1))],
        out_specs=qblk,
        out_shape=jax.ShapeDtypeStruct(q.shape, BF16),
        compiler_params=_cparams(("arbitrary", "arbitrary")),
        name="attn_dec",
    )(q, k.reshape(batch, t, D_MODEL), v.reshape(batch, t, D_MODEL), kc, vc, bias)


def _pffn_kernel(*refs, conf, final, mod_row):
    refs = list(refs)
    x_ref, y_ref, mod_ref, wp_ref = refs[:4]
    pos = 4
    bp_ref = lng_ref = lnb_ref = fin_ref = None
    if conf:
        bp_ref, lng_ref, lnb_ref = refs[pos:pos + 3]
        pos += 3
    w1_ref, w2_ref = refs[pos:pos + 2]
    pos += 2
    if final:
        fin_ref = refs[pos]
        pos += 1
    o_ref, h2_ref, wpb_ref = refs[pos:pos + 3]
    f = pl.program_id(1)

    @pl.when(f == 0)
    def _():
        _, _, g1, sh2, sc2, _ = _mod_parts(mod_ref, mod_row)
        wpb_ref[...] = wp_ref[...].astype(BF16)
        for rb in range(TM // ROW_BLK):
            rows = slice(rb * ROW_BLK, (rb + 1) * ROW_BLK)
            if conf:
                z = y_ref[rows, :]
                mu = jnp.mean(z, axis=-1, keepdims=True)
                zc = z - mu
                var = jnp.mean(zc * zc, axis=-1, keepdims=True)
                zn = zc * lax.rsqrt(var + EPS) * lng_ref[...] + lnb_ref[...]
                y = (zn * _sigmoid(zn)).astype(BF16)
            else:
                y = y_ref[rows, :]
            proj = _dot(y, wpb_ref[...])
            if bp_ref is not None:
                proj = proj + bp_ref[...]
            x1 = x_ref[rows, :] + g1 * proj
            o_ref[rows, :] = x1
            h2_ref[rows, :] = (_rms(x1) * (1.0 + sc2) + sh2).astype(BF16)

    u = _dot(h2_ref[...], w1_ref[...])
    u = jnp.square(jnp.maximum(u, 0.0)).astype(BF16)
    g2 = _mod_parts(mod_ref, mod_row)[5]
    o_ref[...] += g2 * _dot(u, w2_ref[...])

    if final:
        @pl.when(f == pl.num_programs(1) - 1)
        def _():
            o_ref[...] = _rms(o_ref[...]) * fin_ref[...]


def _pffn(x2, y, mods, layer, mod_row, w_proj, j, w1, w2, b_proj=None, ln_g=None, ln_b=None,
          final_g=None):
    rows = x2.shape[0]
    conf = ln_g is not None
    final = final_g is not None
    row_spec = pl.BlockSpec((TM, D_MODEL), lambda i, f: (i, 0))
    vec_spec = pl.BlockSpec((None, 1, D_MODEL), lambda i, f: (j, 0, 0))
    in_specs = [row_spec, row_spec,
                pl.BlockSpec((None, MOD_ROWS, 6 * D_MODEL), lambda i, f: (layer, 0, 0)),
                pl.BlockSpec((None, D_MODEL, D_MODEL), lambda i, f: (j, 0, 0),
                             pipeline_mode=pl.Buffered(1))]
    args = [x2, y, mods, w_proj]
    if conf:
        in_specs += [vec_spec, vec_spec, vec_spec]
        args += [v.reshape(v.shape[0], 1, D_MODEL) for v in (b_proj, ln_g, ln_b)]
    in_specs += [pl.BlockSpec((None, D_MODEL, FK), lambda i, f: (layer, 0, f)),
                 pl.BlockSpec((None, FK, D_MODEL), lambda i, f: (layer, f, 0))]
    args += [w1, w2]
    if final:
        in_specs.append(pl.BlockSpec((1, D_MODEL), lambda i, f: (0, 0)))
        args.append(final_g.reshape(1, D_MODEL))
    return pl.pallas_call(
        functools.partial(_pffn_kernel, conf=conf, final=final, mod_row=mod_row),
        grid=(rows // TM, D_FF // FK),
        in_specs=in_specs,
        out_specs=row_spec,
        out_shape=jax.ShapeDtypeStruct((rows, D_MODEL), F32),
        scratch_shapes=[pltpu.VMEM((TM, D_MODEL), BF16), pltpu.VMEM((D_MODEL, D_MODEL), BF16)],
        compiler_params=_cparams(("parallel", "arbitrary")),
        name="pffn" + ("_conf" if conf else "") + ("_final" if final else ""),
    )(*args)


def _trunk(x, mods, is_ctx, p, state_lru, cache_k, cache_v, bias_tabs):
    bsz, t, d = x.shape
    rows = bsz * t
    assert d == D_MODEL and rows % GROUP_ROWS == 0 and rows % TM == 0
    if is_ctx:
        assert t == PS
    else:
        assert t == GROUP_ROWS == GRID_H * GRID_W and cache_k.shape[2] == PS
    x2 = x.reshape(rows, d)
    depth = mods.shape[0]
    row_mm = _make_mod_row(is_ctx, t, TM_MM)
    row_ffn = _make_mod_row(is_ctx, t, TM)
    row_seq = _make_mod_row(is_ctx, t, GROUP_ROWS)
    states, ks, vs = [], [], []
    for i in range(depth):
        kind, j = i % 3, i // 3
        fin = p["final_g"] if i == depth - 1 else None
        ffn = functools.partial(_pffn, x2, mods=mods, layer=i, mod_row=row_ffn, j=j,
                                w1=p["w_ff1"], w2=p["w_ff2"], final_g=fin)
        if kind == 0:
            y, st = _lru_seq(x2, mods, i, row_seq, p, j, None if is_ctx else state_lru)
            if is_ctx:
                states.append(st)
            x2 = ffn(y=y, w_proj=p["lru_w_out"])
        elif kind == 1:
            zc = _conv_seq(x2, mods, i, row_seq, p, j, chunked=not is_ctx)
            x2 = ffn(y=zc, w_proj=p["conf_w_pw2"], b_proj=p["conf_b_pw2"],
                     ln_g=p["conf_ln_g"], ln_b=p["conf_ln_b"])
        else:
            if is_ctx:
                q, k, v, k_cache, v_cache = _qkv(x2, mods, i, row_mm, p["na_w_qkv"], j, True)
                o = _attn_ctx(q, k, v, bsz)
                ks.append(k_cache.reshape(bsz, t, NA_HEADS, NA_HEAD_DIM))
                vs.append(v_cache.reshape(bsz, t, NA_HEADS, NA_HEAD_DIM))
            else:
                q, k, v = _qkv(x2, mods, i, row_mm, p["na_w_qkv"], j, False)
                kc = cache_k.reshape(cache_k.shape[:3] + (d,))
                vc = cache_v.reshape(cache_v.shape[:3] + (d,))
                o = _attn_dec(q, k, v, kc, vc, j, bias_tabs[j], bsz)
            x2 = ffn(y=o, w_proj=p["na_w_o"])
    return x2.reshape(bsz, t, d), states, ks, vs


def kernel(x_prompt, x_sample, state_lru, cache_k, cache_v, c, c_ctx, w_mod, b_mod, w_ff1, w_ff2, lru_w_in, lru_conv_w, lru_conv_b, lru_w_a, lru_b_a, lru_w_x, lru_b_x, lru_lambda, lru_w_out, conf_w_pw1, conf_b_pw1, conf_dw_w, conf_dw_b, conf_ln_g, conf_ln_b, conf_w_pw2, conf_b_pw2, na_w_qkv, na_w_o, na_rpb, final_g):
    p = dict(w_ff1=_to_bf16(w_ff1), w_ff2=_to_bf16(w_ff2), lru_w_in=lru_w_in, lru_conv_w=lru_conv_w,
             lru_conv_b=lru_conv_b, lru_w_a=lru_w_a, lru_b_a=lru_b_a, lru_w_x=lru_w_x,
             lru_b_x=lru_b_x, lru_lambda=lru_lambda, lru_w_out=lru_w_out,
             conf_w_pw1=conf_w_pw1, conf_b_pw1=conf_b_pw1, conf_dw_w=conf_dw_w,
             conf_dw_b=conf_dw_b, conf_ln_g=conf_ln_g, conf_ln_b=conf_ln_b,
             conf_w_pw2=conf_w_pw2, conf_b_pw2=conf_b_pw2, na_w_qkv=na_w_qkv, na_w_o=na_w_o,
             final_g=final_g)
    dec_b = c.shape[0]
    assert 1 + dec_b <= MOD_ROWS
    cond8 = jnp.concatenate(
        [c_ctx[None, :], c, jnp.zeros((MOD_ROWS - 1 - dec_b, D_MODEL), F32)], axis=0)
    mods = _adaln(cond8, w_mod, b_mod)
    bias_tabs = [_attn_bias_table(na_rpb, j) for j in range(na_rpb.shape[0])]

    y_prompt, states, ks, vs = _trunk(x_prompt, mods, True, p, None, None, None, None)
    y_sample, _, _, _ = _trunk(x_sample, mods, False, p, state_lru, cache_k, cache_v, bias_tabs)

    new_state = jnp.stack([jnp.transpose(s, (1, 0, 2)) for s in states], axis=1)
    new_k = jnp.stack(ks, axis=1)
    new_v = jnp.stack(vs, axis=1)
    return (y_prompt, y_sample, new_state, new_k, new_v)
```

```python
import functools
import math

import jax
import jax.numpy as jnp
from jax import lax
from jax.experimental import pallas as pl
from jax.experimental.pallas import tpu as pltpu

F32 = jnp.float32
BF16 = jnp.bfloat16

D_MODEL = 1024
D_FF = 4 * D_MODEL
PS = 256
CB = 256
NB = D_MODEL // CB
SLOTS = 8
GROUP_ROWS = SLOTS * PS
GRID_W = 64
GRID_H = 32
NA_HEADS = 16
NA_HEAD_DIM = 64
NA_WIN_ROWS = 8
NA_WIN_COLS = 16
ATT_SCALE = NA_HEAD_DIM ** -0.5
assert math.frexp(ATT_SCALE)[0] == 0.5, "the attention kernels scale bf16 queries exactly"
LRU_C = 8.0
LRU_CONV_W = 4
LOG2_E = math.log2(math.e)
CONF_CONV_W = 31
EPS = 1e-6
NEG_BIG = -1e30

LANES = 128
SUBLANES = 8
MOD_ROWS = SUBLANES
TM = 1024
TM_MM = 512
ROW_BLK = 256
FK = 2048
FK_CONF = 1024
CAST_TILE = 1024
MOD_TILE = 1536
Q_ROWS = 4
Q_BLK = Q_ROWS * GRID_W
N_QBLK = GRID_H // Q_ROWS
WIN_ROWS_BLK = 12
WIN_KEYS = WIN_ROWS_BLK * GRID_W
DEC_SLABS = 8
VMEM_LIMIT = 56 * 1024 * 1024


def _cparams(sem):
    return pltpu.CompilerParams(dimension_semantics=sem, vmem_limit_bytes=VMEM_LIMIT)


def _dot(a, b):
    return jnp.dot(a, b, preferred_element_type=F32)


def _dot_t(a, b):
    return lax.dot_general(a, b, (((1,), (1,)), ((), ())), preferred_element_type=F32)


def _rms(x):
    return x * lax.rsqrt(jnp.mean(x * x, axis=-1, keepdims=True) + EPS)


def _sigmoid(x):
    return 0.5 * jnp.tanh(0.5 * x) + 0.5


def _mod_parts(mod_ref, mod_row):
    m = mod_ref[pl.ds(mod_row(pl.program_id(0)), 1), :]
    return [m[:, k * D_MODEL:(k + 1) * D_MODEL] for k in range(6)]


def _make_mod_row(is_ctx, seq_len, tile_rows):
    if is_ctx:
        return lambda i: 0
    return lambda i: 1 + (i * tile_rows) // seq_len


def _adaln_kernel(c_ref, w_ref, b_ref, o_ref):
    c = c_ref[...]
    s = (c * jax.nn.sigmoid(c)).astype(BF16)
    o_ref[...] = _dot(s, w_ref[...].astype(BF16)) + b_ref[...]


def _adaln(cond8, w_mod, b_mod):
    depth = w_mod.shape[0]
    n_out = w_mod.shape[2]
    return pl.pallas_call(
        _adaln_kernel,
        grid=(depth, n_out // MOD_TILE),
        in_specs=[
            pl.BlockSpec((MOD_ROWS, D_MODEL), lambda l, n: (0, 0)),
            pl.BlockSpec((None, D_MODEL, MOD_TILE), lambda l, n: (l, 0, n)),
            pl.BlockSpec((None, 1, MOD_TILE), lambda l, n: (l, 0, n)),
        ],
        out_specs=pl.BlockSpec((None, MOD_ROWS, MOD_TILE), lambda l, n: (l, 0, n)),
        out_shape=jax.ShapeDtypeStruct((depth, MOD_ROWS, n_out), F32),
        compiler_params=_cparams(("parallel", "parallel")),
        name="adaln",
    )(cond8, w_mod, b_mod.reshape(depth, 1, n_out))


def _cast_kernel(w_ref, o_ref):
    o_ref[...] = w_ref[...].astype(o_ref.dtype)


def _to_bf16(w):
    nl, a, b = w.shape
    blk = pl.BlockSpec((None, CAST_TILE, CAST_TILE), lambda l, i, k: (l, i, k))
    return pl.pallas_call(
        _cast_kernel,
        grid=(nl, a // CAST_TILE, b // CAST_TILE),
        in_specs=[blk],
        out_specs=blk,
        out_shape=jax.ShapeDtypeStruct(w.shape, BF16),
        compiler_params=_cparams(("parallel", "parallel", "parallel")),
        name="cast_bf16",
    )(w)


def _modulated(x_ref, mod_ref, mod_row):
    sh1, sc1 = _mod_parts(mod_ref, mod_row)[:2]
    return (_rms(x_ref[...]) * (1.0 + sc1) + sh1).astype(BF16)


def _qkv_kernel(x_ref, mod_ref, w_ref, *refs, mod_row, with_cache):
    qkv_refs, wbf_ref = refs[:3], refs[-1]

    @pl.when(pl.program_id(0) == 0)
    def _():
        wbf_ref[...] = w_ref[...].astype(BF16)

    sh1, sc1 = _mod_parts(mod_ref, mod_row)[:2]
    for rb in range(TM_MM // ROW_BLK):
        rows = slice(rb * ROW_BLK, (rb + 1) * ROW_BLK)
        h = (_rms(x_ref[rows, :]) * (1.0 + sc1) + sh1).astype(BF16)
        for g, o_ref in enumerate(qkv_refs):
            o = _dot(h, wbf_ref[:, g * D_MODEL:(g + 1) * D_MODEL])
            o_ref[rows, :] = o.astype(o_ref.dtype)
            if with_cache and g > 0:
                refs[2 + g][rows] = o.reshape(ROW_BLK, NA_HEADS, NA_HEAD_DIM)


def _qkv(x2, mods, layer, mod_row, w, j, with_cache):
    rows = x2.shape[0]
    n_cols = w.shape[2]
    row_spec = pl.BlockSpec((TM_MM, D_MODEL), lambda i: (i, 0))
    out_specs = [row_spec] * 3
    out_shape = [jax.ShapeDtypeStruct((rows, D_MODEL), BF16)] * 3
    if with_cache:
        out_specs += [pl.BlockSpec((TM_MM, NA_HEADS, NA_HEAD_DIM), lambda i: (i, 0, 0))] * 2
        out_shape += [jax.ShapeDtypeStruct((rows, NA_HEADS, NA_HEAD_DIM), F32)] * 2
    return pl.pallas_call(
        functools.partial(_qkv_kernel, mod_row=mod_row, with_cache=with_cache),
        grid=(rows // TM_MM,),
        in_specs=[row_spec,
                  pl.BlockSpec((None, MOD_ROWS, 6 * D_MODEL), lambda i: (layer, 0, 0)),
                  pl.BlockSpec((None, D_MODEL, n_cols), lambda i: (j, 0, 0),
                               pipeline_mode=pl.Buffered(1))],
        out_specs=out_specs,
        out_shape=out_shape,
        scratch_shapes=[pltpu.VMEM((D_MODEL, n_cols), BF16)],
        compiler_params=_cparams(("arbitrary",)),
        name="modmm_qkv",
    )(x2, mods, w)


def _to_time_major(x2):
    return jnp.swapaxes(x2.reshape(SLOTS, PS, CB), 0, 1)


def _from_time_major(x3):
    return jnp.swapaxes(x3, 0, 1).reshape(GROUP_ROWS, CB)


def _slot_iota():
    return lax.broadcasted_iota(jnp.int32, (SLOTS, CB), 0)


def _from_prev_slot(tile):
    return jnp.where(_slot_iota() == 0, 0.0, pltpu.roll(tile, 1, 0))


def _from_next_slot(tile):
    return jnp.where(_slot_iota() == SLOTS - 1, 0.0, pltpu.roll(tile, SLOTS - 1, 0))


def _fill_padded(pad_ref, x, lo, hi, chunked):
    pad_ref[lo:lo + PS] = _to_time_major(x)
    for r in range(lo):
        if chunked:
            pad_ref[r] = _from_prev_slot(pad_ref[PS + r])
        else:
            pad_ref[r] = jnp.zeros((SLOTS, CB), F32)
    for r in range(hi):
        if chunked:
            pad_ref[lo + PS + r] = _from_next_slot(pad_ref[lo + r])
        else:
            pad_ref[lo + PS + r] = jnp.zeros((SLOTS, CB), F32)


def _group_spec():
    return pl.BlockSpec((GROUP_ROWS, CB), lambda g, n: (g, n))


def _chan_spec(lead, j, col0=0):
    return pl.BlockSpec((None, lead, CB), lambda g, n: (j, 0, col0 + n))


def _seq_in_specs(layer):
    return [pl.BlockSpec((GROUP_ROWS, D_MODEL), lambda g, n: (g, 0)),
            pl.BlockSpec((None, MOD_ROWS, 6 * D_MODEL), lambda g, n: (layer, 0, 0))]


def _seq_modulated(x_ref, mod_ref, h_ref, mod_row):
    @pl.when(pl.program_id(1) == 0)
    def _():
        h_ref[...] = _modulated(x_ref, mod_ref, mod_row)


LRU_TC = 128


def _softplus(x):
    return jnp.maximum(x, 0.0) + jnp.log1p(jnp.exp(-jnp.abs(x)))


def _lru_seq_kernel(*refs, chunked, mod_row):
    if chunked:
        (x_ref, mod_ref, wg_ref, wr_ref, cw_ref, cb_ref, wa_ref, ba_ref, wx_ref, bx_ref, lam_ref,
         h0_ref, y_ref, h_ref, gate_ref, pad_ref, af_ref, bf_ref, ab_ref, bb_ref, wbf_ref) = refs
    else:
        (x_ref, mod_ref, wg_ref, wr_ref, cw_ref, cb_ref, wa_ref, ba_ref, wx_ref, bx_ref, lam_ref,
         y_ref, st_ref, h_ref, gate_ref, pad_ref, af_ref, bf_ref, ab_ref, bb_ref, wbf_ref) = refs
    lo = (LRU_CONV_W - 1) // 2
    hi = LRU_CONV_W - 1 - lo
    _seq_modulated(x_ref, mod_ref, h_ref, mod_row)
    h = h_ref[...]
    gate_ref[...] = jax.nn.gelu(_dot(h, wg_ref[...].astype(BF16)), approximate=True)
    _fill_padded(pad_ref, _dot(h, wr_ref[...].astype(BF16)), lo, hi, chunked)

    a_refs = (af_ref, ab_ref)
    b_refs = (bf_ref, bb_ref)
    hc = [0.5 * LRU_C * _softplus(-lam_ref[d:d + 1, :]) for d in range(2)]
    for d in range(2):
        wbf_ref[2 * d] = (0.5 * wa_ref[d]).astype(BF16)
        wbf_ref[2 * d + 1] = (0.5 * wx_ref[d]).astype(BF16)
    hba = [0.5 * ba_ref[d:d + 1, :] for d in range(2)]
    hbx = [0.5 * bx_ref[d:d + 1, :] for d in range(2)]

    def gates(ci, carry):
        t0 = pl.multiple_of(ci * LRU_TC, LRU_TC)
        xf = cb_ref[...] + cw_ref[0:1, :] * pad_ref[pl.ds(t0, LRU_TC)]
        for k in range(1, LRU_CONV_W):
            xf = xf + cw_ref[k:k + 1, :] * pad_ref[pl.ds(t0 + k, LRU_TC)]
        x2 = xf.reshape(LRU_TC * SLOTS, CB)
        xb = x2.astype(BF16)
        hx = 0.5 * x2
        for d in range(2):
            tr = jnp.tanh(_dot(xb, wbf_ref[2 * d]) + hba[d])
            ti = jnp.tanh(_dot(xb, wbf_ref[2 * d + 1]) + hbx[d])
            pos = hc[d] * tr + hc[d]
            a = jnp.exp2(pos * (-LOG2_E))
            one_m_a2 = jnp.tanh(pos) * (a * a + 1.0)
            root = jnp.where(one_m_a2 > 0.0, one_m_a2 * lax.rsqrt(one_m_a2), 0.0)
            bx = root * (hx * ti + hx)
            a_refs[d][pl.ds(t0, LRU_TC)] = a.reshape(LRU_TC, SLOTS, CB)
            b_refs[d][pl.ds(t0, LRU_TC)] = bx.reshape(LRU_TC, SLOTS, CB)
        return carry

    lax.fori_loop(0, PS // LRU_TC, gates, 0)

    zero = jnp.zeros((SLOTS, CB), F32)
    one = jnp.ones((SLOTS, CB), F32)

    def two_steps(a_ref, b_ref, t0, t1, h, p):
        a0, a1 = a_ref[t0], a_ref[t1]
        b0, b1 = b_ref[t0], b_ref[t1]
        a01 = a1 * a0
        b_ref[t0] = a0 * h + b0
        h = a01 * h + (a1 * b0 + b1)
        b_ref[t1] = h
        if chunked:
            a_ref[t0] = a0 * p
            p = a01 * p
            a_ref[t1] = p
        return h, p

    def scan(i, carry):
        hf, hb, pf, pb = carry
        t = 2 * i
        hf, pf = two_steps(af_ref, bf_ref, t, t + 1, hf, pf)
        hb, pb = two_steps(ab_ref, bb_ref, PS - 1 - t, PS - 2 - t, hb, pb)
        return hf, hb, pf, pb

    lax.fori_loop(0, PS // 2, scan, (zero, zero, one, one), unroll=2)

    if chunked:
        slot = _slot_iota()
        h0f = jnp.broadcast_to(h0_ref[0:1, :], (SLOTS, CB))
        h0b = jnp.broadcast_to(h0_ref[1:2, :], (SLOTS, CB))
        end_f, prod_f = bf_ref[PS - 1], af_ref[PS - 1]
        end_b, prod_b = bb_ref[0], ab_ref[0]
        in_f = jnp.where(slot == 0, h0f, 0.0)
        in_b = jnp.where(slot == SLOTS - 1, h0b, 0.0)
        for _ in range(SLOTS - 1):
            in_f = jnp.where(slot == 0, h0f, pltpu.roll(end_f + prod_f * in_f, 1, 0))
            in_b = jnp.where(slot == SLOTS - 1, h0b,
                             pltpu.roll(end_b + prod_b * in_b, SLOTS - 1, 0))
    else:
        st_ref[0] = bf_ref[PS - 1]
        st_ref[1] = bb_ref[0]

    def combine(ci, carry):
        sl = pl.ds(pl.multiple_of(ci * LRU_TC, LRU_TC), LRU_TC)
        hs = bf_ref[sl] + bb_ref[sl]
        if chunked:
            hs = hs + af_ref[sl] * in_f + ab_ref[sl] * in_b
        bf_ref[sl] = hs
        return carry

    lax.fori_loop(0, PS // LRU_TC, combine, 0)
    y_ref[...] = (_from_time_major(bf_ref[...]) * gate_ref[...]).astype(y_ref.dtype)


def _lru_seq(x2, mods, layer, mod_row, p, j, state_lru=None):
    chunked = state_lru is not None
    rows = x2.shape[0]
    groups = rows // GROUP_ROWS
    n_layers = p["lru_conv_b"].shape[0]
    seq = _group_spec()
    wblk = pl.BlockSpec((None, 2, None, CB, CB), lambda g, n: (j, 0, n, 0, 0))
    in_specs = _seq_in_specs(layer) + [
        _chan_spec(D_MODEL, j), _chan_spec(D_MODEL, j, NB),
        _chan_spec(LRU_CONV_W, j), _chan_spec(1, j),
        wblk, _chan_spec(2, j), wblk, _chan_spec(2, j), _chan_spec(2, j)]
    args = [x2, mods] + [p["lru_w_in"]] * 2 + [
            p["lru_conv_w"], p["lru_conv_b"].reshape(n_layers, 1, D_MODEL),
            p["lru_w_a"], p["lru_b_a"], p["lru_w_x"], p["lru_b_x"], p["lru_lambda"]]
    y_shape = jax.ShapeDtypeStruct((rows, D_MODEL), BF16)
    if chunked:
        in_specs.append(pl.BlockSpec((None, None, 2, CB), lambda g, n: (g, j, 0, n)))
        args.append(state_lru)
        out_specs = [seq]
        out_shape = [y_shape]
    else:
        out_specs = [seq, pl.BlockSpec((2, SLOTS, CB), lambda g, n: (0, g, n))]
        out_shape = [y_shape, jax.ShapeDtypeStruct((2, groups * SLOTS, D_MODEL), F32)]
    tile = (PS, SLOTS, CB)
    outs = pl.pallas_call(
        functools.partial(_lru_seq_kernel, chunked=chunked, mod_row=mod_row),
        grid=(groups, NB),
        in_specs=in_specs,
        out_specs=out_specs,
        out_shape=out_shape,
        scratch_shapes=[pltpu.VMEM((GROUP_ROWS, D_MODEL), BF16), pltpu.VMEM((GROUP_ROWS, CB), F32),
                        pltpu.VMEM((PS + LRU_CONV_W - 1, SLOTS, CB), F32)]
        + [pltpu.VMEM(tile, F32) for _ in range(4)] + [pltpu.VMEM((4, CB, CB), BF16)],
        compiler_params=_cparams(("parallel", "arbitrary")),
        name="lru_seq_dec" if chunked else "lru_seq_ctx",
    )(*args)
    return (outs[0], None) if chunked else (outs[0], outs[1])


CONV_TC = 128


def _conv_seq_kernel(x_ref, mod_ref, wv_ref, wg_ref, bv_ref, bg_ref, w_ref, b_ref, o_ref,
                     h_ref, pad_ref, out_ref, *, chunked, mod_row):
    lo = (CONF_CONV_W - 1) // 2
    hi = CONF_CONV_W - 1 - lo
    _seq_modulated(x_ref, mod_ref, h_ref, mod_row)
    h = h_ref[...]
    val = _dot(h, wv_ref[...].astype(BF16)) + bv_ref[...]
    gate = _dot(h, wg_ref[...].astype(BF16)) + bg_ref[...]
    _fill_padded(pad_ref, val * _sigmoid(gate), lo, hi, chunked)

    def chunk(ci, carry):
        t0 = pl.multiple_of(ci * CONV_TC, CONV_TC)
        acc = b_ref[...] + w_ref[0:1, :] * pad_ref[pl.ds(t0, CONV_TC)]
        for k in range(1, CONF_CONV_W):
            acc = acc + w_ref[k:k + 1, :] * pad_ref[pl.ds(t0 + k, CONV_TC)]
        out_ref[pl.ds(t0, CONV_TC)] = acc
        return carry

    lax.fori_loop(0, PS // CONV_TC, chunk, 0)
    o_ref[...] = _from_time_major(out_ref[...])


def _conv_seq(x2, mods, layer, mod_row, p, j, chunked):
    rows = x2.shape[0]
    n_layers = p["conf_dw_b"].shape[0]
    w_pw1 = p["conf_w_pw1"]
    b_pw1 = p["conf_b_pw1"].reshape(n_layers, 1, 2 * D_MODEL)
    return pl.pallas_call(
        functools.partial(_conv_seq_kernel, chunked=chunked, mod_row=mod_row),
        grid=(rows // GROUP_ROWS, NB),
        in_specs=_seq_in_specs(layer) + [
            _chan_spec(D_MODEL, j), _chan_spec(D_MODEL, j, NB),
            _chan_spec(1, j), _chan_spec(1, j, NB),
            _chan_spec(CONF_CONV_W, j), _chan_spec(1, j)],
        out_specs=_group_spec(),
        out_shape=jax.ShapeDtypeStruct((rows, D_MODEL), F32),
        scratch_shapes=[pltpu.VMEM((GROUP_ROWS, D_MODEL), BF16),
                        pltpu.VMEM((PS + CONF_CONV_W - 1, SLOTS, CB), F32),
                        pltpu.VMEM((PS, SLOTS, CB), F32)],
        compiler_params=_cparams(("parallel", "arbitrary")),
        name="conv_seq_dec" if chunked else "conv_seq_ctx",
    )(x2, mods, w_pw1, w_pw1, b_pw1, b_pw1, p["conf_dw_w"],
      p["conf_dw_b"].reshape(n_layers, 1, D_MODEL))


def _attend(q, parts):
    lane = lax.broadcasted_iota(jnp.int32, (1, LANES), 1)
    m = q.shape[0]
    zero = jnp.zeros_like(q)
    qm = jnp.concatenate([jnp.where(lane < NA_HEAD_DIM, q, zero),
                          jnp.where(lane >= NA_HEAD_DIM, q, zero)], axis=0) * ATT_SCALE
    scores = []
    for k, _, bias in parts:
        s = _dot_t(qm, k)
        if bias is not None:
            ref, head = bias
            s = s + jnp.concatenate([ref[head], ref[head + 1]], axis=0)
        scores.append(s)
    mx = scores[0].max(axis=-1, keepdims=True)
    for s in scores[1:]:
        mx = jnp.maximum(mx, s.max(axis=-1, keepdims=True))
    den = 0.0
    acc = 0.0
    for s, (_, v, _) in zip(scores, parts):
        pr = jnp.exp(s - mx)
        den = den + pr.sum(axis=-1, keepdims=True)
        acc = acc + _dot(pr.astype(BF16), v)
    o = acc / den
    return jnp.where(lane < NA_HEAD_DIM, o[:m], o[m:])


def _attn_ctx_kernel(q_ref, k_ref, v_ref, o_ref):
    for s in range(D_MODEL // LANES):
        sl = slice(s * LANES, (s + 1) * LANES)
        k = k_ref[:, sl].astype(BF16)
        v = v_ref[:, sl].astype(BF16)
        o_ref[:, sl] = _attend(q_ref[:, sl], [(k, v, None)]).astype(o_ref.dtype)


def _attn_ctx(q, k, v, batch):
    blk = pl.BlockSpec((PS, D_MODEL), lambda b: (b, 0))
    return pl.pallas_call(
        _attn_ctx_kernel,
        grid=(batch,),
        in_specs=[blk, blk, blk],
        out_specs=blk,
        out_shape=jax.ShapeDtypeStruct(q.shape, BF16),
        compiler_params=_cparams(("parallel",)),
        name="attn_ctx",
    )(q, k, v)


def _attn_dec_kernel(q_ref, k_ref, v_ref, kc_ref, vc_ref, bias_ref, o_ref):
    i = pl.program_id(0)
    row0 = jnp.clip(Q_ROWS * i - NA_WIN_ROWS // 2, 0, GRID_H - WIN_ROWS_BLK)
    win = pl.ds(pl.multiple_of(row0 * GRID_W, GRID_W), WIN_KEYS)
    for s in range(DEC_SLABS):
        sl = slice(s * LANES, (s + 1) * LANES)
        kc = kc_ref[:, sl].astype(BF16)
        vc = vc_ref[:, sl].astype(BF16)
        o = _attend(q_ref[:, sl], [(k_ref[win, sl], v_ref[win, sl], (bias_ref, 2 * s)),
                                   (kc, vc, None)])
        o_ref[:, sl] = o.astype(o_ref.dtype)


def _qblk_window(i):
    row0 = min(max(Q_ROWS * i - NA_WIN_ROWS // 2, 0), GRID_H - WIN_ROWS_BLK)
    out = []
    for a in range(Q_ROWS):
        r = Q_ROWS * i + a
        rs = min(max(r - NA_WIN_ROWS // 2, 0), GRID_H - NA_WIN_ROWS)
        out.append((r, [rs <= row0 + w < rs + NA_WIN_ROWS for w in range(WIN_ROWS_BLK)]))
    return row0, out


BIAS_CLASSES = (0, 1, N_QBLK - 1)


def _bias_kernel(rpb_ref, o_ref):
    c = lax.broadcasted_iota(jnp.int32, (GRID_W, LANES), 0)
    l = lax.broadcasted_iota(jnp.int32, (GRID_W, LANES), 1)
    cs = jnp.clip(c - NA_WIN_COLS // 2, 0, GRID_W - NA_WIN_COLS)
    in_cols = (l >= cs) & (l < cs + NA_WIN_COLS)
    neg = jnp.full((GRID_W, LANES), NEG_BIG, F32)
    lo_half, hi_half = [], []
    for dr in range(2 * NA_WIN_ROWS - 1):
        row = jnp.broadcast_to(rpb_ref[dr:dr + 1, :], (GRID_W, LANES))
        t = pltpu.roll(row, LANES - (NA_WIN_COLS - 1), 1, stride=1, stride_axis=0)
        t = jnp.where(in_cols, t, NEG_BIG)
        lo_half.append(t)
        hi_half.append(pltpu.roll(t, GRID_W, 1))
    for cls, i in enumerate(BIAS_CLASSES):
        row0, qrows = _qblk_window(i)
        for a, (r, valid) in enumerate(qrows):
            for wp in range(WIN_ROWS_BLK // 2):
                halves = []
                for half, bank in enumerate((lo_half, hi_half)):
                    w = 2 * wp + half
                    halves.append(bank[row0 + w - r + NA_WIN_ROWS - 1] if valid[w] else neg)
                o_ref[cls, a * GRID_W:(a + 1) * GRID_W, wp * LANES:(wp + 1) * LANES] = (
                    jnp.where(l < GRID_W, halves[0], halves[1]))


def _attn_bias_table(rpb, j):
    nl, nh, ndr, ndc = rpb.shape
    ndr_pad = -(-ndr // SUBLANES) * SUBLANES
    rpb_p = jnp.pad(rpb, ((0, 0), (0, 0), (0, ndr_pad - ndr), (0, LANES - ndc)))
    return pl.pallas_call(
        _bias_kernel,
        grid=(nh,),
        in_specs=[pl.BlockSpec((None, None, ndr_pad, LANES), lambda h: (j, h, 0, 0))],
        out_specs=pl.BlockSpec((len(BIAS_CLASSES), None, Q_BLK, WIN_KEYS), lambda h: (0, h, 0, 0)),
        out_shape=jax.ShapeDtypeStruct((len(BIAS_CLASSES), nh, Q_BLK, WIN_KEYS), F32),
        compiler_params=_cparams(("parallel",)),
        name="attn_bias",
    )(rpb_p)


def _attn_dec(q, k, v, kc, vc, j, bias, batch):
    t = GRID_H * GRID_W
    assert DEC_SLABS * LANES == D_MODEL
    qblk = pl.BlockSpec((Q_BLK, D_MODEL), lambda i, b: (b * N_QBLK + i, 0))
    kvblk = pl.BlockSpec((None, t, D_MODEL), lambda i, b: (b, 0, 0))
    cblk = pl.BlockSpec((None, None, PS, D_MODEL), lambda i, b: (b, j, 0, 0))

    def bias_idx(i, b):
        cls = (i > 0).astype(jnp.int32) + (i == N_QBLK - 1).astype(jnp.int32)
        return (cls, 0, 0, 0)

    return pl.pallas_call(
        _attn_dec_kernel,
        grid=(N_QBLK, batch),
        in_specs=[qblk, kvblk, kvblk, cblk, cblk,
                  pl.BlockSpec((None, NA_HEADS, Q_BLK, WIN_KEYS), bias_idx,
                               pipeline_mode=pl.Buffered(1))],
        out_specs=qblk,
        out_shape=jax.ShapeDtypeStruct(q.shape, BF16),
        compiler_params=_cparams(("arbitrary", "arbitrary")),
        name="attn_dec",
    )(q, k.reshape(batch, t, D_MODEL), v.reshape(batch, t, D_MODEL), kc, vc, bias)


def _pffn_kernel(*refs, conf, final, mod_row):
    refs = list(refs)
    x_ref, y_ref, mod_ref, wp_ref = refs[:4]
    pos = 4
    bp_ref = lng_ref = lnb_ref = fin_ref = None
    if conf:
        bp_ref, lng_ref, lnb_ref = refs[pos:pos + 3]
        pos += 3
    w1_ref, w2_ref = refs[pos:pos + 2]
    pos += 2
    if final:
        fin_ref = refs[pos]
        pos += 1
    o_ref, h2_ref, wpb_ref = refs[pos:pos + 3]
    f = pl.program_id(1)

    @pl.when(f == 0)
    def _():
        _, _, g1, sh2, sc2, _ = _mod_parts(mod_ref, mod_row)
        wpb_ref[...] = wp_ref[...].astype(BF16)
        for rb in range(TM // ROW_BLK):
            rows = slice(rb * ROW_BLK, (rb + 1) * ROW_BLK)
            if conf:
                z = y_ref[rows, :]
                mu = jnp.mean(z, axis=-1, keepdims=True)
                zc = z - mu
                var = jnp.mean(zc * zc, axis=-1, keepdims=True)
                zn = zc * lax.rsqrt(var + EPS) * lng_ref[...] + lnb_ref[...]
                y = (zn * _sigmoid(zn)).astype(BF16)
            else:
                y = y_ref[rows, :]
            proj = _dot(y, wpb_ref[...])
            if bp_ref is not None:
                proj = proj + bp_ref[...]
            x1 = x_ref[rows, :] + g1 * proj
            o_ref[rows, :] = x1
            h2_ref[rows, :] = (_rms(x1) * (1.0 + sc2) + sh2).astype(BF16)

    u = _dot(h2_ref[...], w1_ref[...])
    u = jnp.square(jnp.maximum(u, 0.0)).astype(BF16)
    g2 = _mod_parts(mod_ref, mod_row)[5]
    o_ref[...] += g2 * _dot(u, w2_ref[...])

    if final:
        @pl.when(f == pl.num_programs(1) - 1)
        def _():
            o_ref[...] = _rms(o_ref[...]) * fin_ref[...]


def _pffn(x2, y, mods, layer, mod_row, w_proj, j, w1, w2, b_proj=None, ln_g=None, ln_b=None,
          final_g=None):
    rows = x2.shape[0]
    conf = ln_g is not None
    final = final_g is not None
    row_spec = pl.BlockSpec((TM, D_MODEL), lambda i, f: (i, 0))
    vec_spec = pl.BlockSpec((None, 1, D_MODEL), lambda i, f: (j, 0, 0))
    in_specs = [row_spec, row_spec,
                pl.BlockSpec((None, MOD_ROWS, 6 * D_MODEL), lambda i, f: (layer, 0, 0)),
                pl.BlockSpec((None, D_MODEL, D_MODEL), lambda i, f: (j, 0, 0),
                             pipeline_mode=pl.Buffered(1))]
    args = [x2, y, mods, w_proj]
    if conf:
        in_specs += [vec_spec, vec_spec, vec_spec]
        args += [v.reshape(v.shape[0], 1, D_MODEL) for v in (b_proj, ln_g, ln_b)]
    fk = FK_CONF if conf else FK
    in_specs += [pl.BlockSpec((None, D_MODEL, fk), lambda i, f: (layer, 0, f)),
                 pl.BlockSpec((None, fk, D_MODEL), lambda i, f: (layer, f, 0))]
    args += [w1, w2]
    if final:
        in_specs.append(pl.BlockSpec((1, D_MODEL), lambda i, f: (0, 0)))
        args.append(final_g.reshape(1, D_MODEL))
    return pl.pallas_call(
        functools.partial(_pffn_kernel, conf=conf, final=final, mod_row=mod_row),
        grid=(rows // TM, D_FF // fk),
        in_specs=in_specs,
        out_specs=row_spec,
        out_shape=jax.ShapeDtypeStruct((rows, D_MODEL), F32),
        scratch_shapes=[pltpu.VMEM((TM, D_MODEL), BF16), pltpu.VMEM((D_MODEL, D_MODEL), BF16)],
        compiler_params=_cparams(("parallel", "arbitrary")),
        name="pffn" + ("_conf" if conf else "") + ("_final" if final else ""),
    )(*args)


def _trunk(x, mods, is_ctx, p, state_lru, cache_k, cache_v, bias_tabs):
    bsz, t, d = x.shape
    rows = bsz * t
    assert d == D_MODEL and rows % GROUP_ROWS == 0 and rows % TM == 0
    if is_ctx:
        assert t == PS
    else:
        assert t == GROUP_ROWS == GRID_H * GRID_W and cache_k.shape[2] == PS
    x2 = x.reshape(rows, d)
    depth = mods.shape[0]
    row_mm = _make_mod_row(is_ctx, t, TM_MM)
    row_ffn = _make_mod_row(is_ctx, t, TM)
    row_seq = _make_mod_row(is_ctx, t, GROUP_ROWS)
    states, ks, vs = [], [], []
    for i in range(depth):
        kind, j = i % 3, i // 3
        fin = p["final_g"] if i == depth - 1 else None
        ffn = functools.partial(_pffn, x2, mods=mods, layer=i, mod_row=row_ffn, j=j,
                                w1=p["w_ff1"], w2=p["w_ff2"], final_g=fin)
        if kind == 0:
            y, st = _lru_seq(x2, mods, i, row_seq, p, j, None if is_ctx else state_lru)
            if is_ctx:
                states.append(st)
            x2 = ffn(y=y, w_proj=p["lru_w_out"])
        elif kind == 1:
            zc = _conv_seq(x2, mods, i, row_seq, p, j, chunked=not is_ctx)
            x2 = ffn(y=zc, w_proj=p["conf_w_pw2"], b_proj=p["conf_b_pw2"],
                     ln_g=p["conf_ln_g"], ln_b=p["conf_ln_b"])
        else:
            if is_ctx:
                q, k, v, k_cache, v_cache = _qkv(x2, mods, i, row_mm, p["na_w_qkv"], j, True)
                o = _attn_ctx(q, k, v, bsz)
                ks.append(k_cache.reshape(bsz, t, NA_HEADS, NA_HEAD_DIM))
                vs.append(v_cache.reshape(bsz, t, NA_HEADS, NA_HEAD_DIM))
            else:
                q, k, v = _qkv(x2, mods, i, row_mm, p["na_w_qkv"], j, False)
                kc = cache_k.reshape(cache_k.shape[:3] + (d,))
                vc = cache_v.reshape(cache_v.shape[:3] + (d,))
                o = _attn_dec(q, k, v, kc, vc, j, bias_tabs[j], bsz)
            x2 = ffn(y=o, w_proj=p["na_w_o"])
    return x2.reshape(bsz, t, d), states, ks, vs


def kernel(x_prompt, x_sample, state_lru, cache_k, cache_v, c, c_ctx, w_mod, b_mod, w_ff1, w_ff2, lru_w_in, lru_conv_w, lru_conv_b, lru_w_a, lru_b_a, lru_w_x, lru_b_x, lru_lambda, lru_w_out, conf_w_pw1, conf_b_pw1, conf_dw_w, conf_dw_b, conf_ln_g, conf_ln_b, conf_w_pw2, conf_b_pw2, na_w_qkv, na_w_o, na_rpb, final_g):
    p = dict(w_ff1=_to_bf16(w_ff1), w_ff2=_to_bf16(w_ff2), lru_w_in=lru_w_in, lru_conv_w=lru_conv_w,
             lru_conv_b=lru_conv_b, lru_w_a=lru_w_a, lru_b_a=lru_b_a, lru_w_x=lru_w_x,
             lru_b_x=lru_b_x, lru_lambda=lru_lambda, lru_w_out=lru_w_out,
             conf_w_pw1=conf_w_pw1, conf_b_pw1=conf_b_pw1, conf_dw_w=conf_dw_w,
             conf_dw_b=conf_dw_b, conf_ln_g=conf_ln_g, conf_ln_b=conf_ln_b,
             conf_w_pw2=conf_w_pw2, conf_b_pw2=conf_b_pw2, na_w_qkv=na_w_qkv, na_w_o=na_w_o,
             final_g=final_g)
    dec_b = c.shape[0]
    assert 1 + dec_b <= MOD_ROWS
    cond8 = jnp.concatenate(
        [c_ctx[None, :], c, jnp.zeros((MOD_ROWS - 1 - dec_b, D_MODEL), F32)], axis=0)
    mods = _adaln(cond8, w_mod, b_mod)
    bias_tabs = [_attn_bias_table(na_rpb, j) for j in range(na_rpb.shape[0])]

    y_prompt, states, ks, vs = _trunk(x_prompt, mods, True, p, None, None, None, None)
    y_sample, _, _, _ = _trunk(x_sample, mods, False, p, state_lru, cache_k, cache_v, bias_tabs)

    new_state = jnp.stack([jnp.transpose(s, (1, 0, 2)) for s in states], axis=1)
    new_k = jnp.stack(ks, axis=1)
    new_v = jnp.stack(vs, axis=1)
    return (y_prompt, y_sample, new_state, new_k, new_v)
```

```python
import functools
import math

import jax
import jax.numpy as jnp
from jax import lax
from jax.experimental import pallas as pl
from jax.experimental.pallas import tpu as pltpu

F32 = jnp.float32
BF16 = jnp.bfloat16

D_MODEL = 1024
D_FF = 4 * D_MODEL
PS = 256
CB = 256
NB = D_MODEL // CB
SLOTS = 8
GROUP_ROWS = SLOTS * PS
GRID_W = 64
GRID_H = 32
NA_HEADS = 16
NA_HEAD_DIM = 64
NA_WIN_ROWS = 8
NA_WIN_COLS = 16
ATT_SCALE = NA_HEAD_DIM ** -0.5
assert math.frexp(ATT_SCALE)[0] == 0.5, "the attention kernels scale bf16 queries exactly"
LRU_C = 8.0
LRU_CONV_W = 4
LOG2_E = math.log2(math.e)
CONF_CONV_W = 31
EPS = 1e-6
NEG_BIG = -1e30

LANES = 128
SUBLANES = 8
MOD_ROWS = SUBLANES
TM = 1024
TM_MM = 512
ROW_BLK = 256
FK = 2048
FK_CONF = 1024
CAST_TILE = 1024
MOD_TILE = 1536
Q_ROWS = 4
Q_BLK = Q_ROWS * GRID_W
N_QBLK = GRID_H // Q_ROWS
WIN_ROWS_BLK = 12
WIN_KEYS = WIN_ROWS_BLK * GRID_W
DEC_SLABS = 8
VMEM_LIMIT = 56 * 1024 * 1024


def _cparams(sem):
    return pltpu.CompilerParams(dimension_semantics=sem, vmem_limit_bytes=VMEM_LIMIT)


def _dot(a, b):
    return jnp.dot(a, b, preferred_element_type=F32)


def _dot_t(a, b):
    return lax.dot_general(a, b, (((1,), (1,)), ((), ())), preferred_element_type=F32)


def _rms(x):
    return x * lax.rsqrt(jnp.mean(x * x, axis=-1, keepdims=True) + EPS)


def _sigmoid(x):
    return 0.5 * jnp.tanh(0.5 * x) + 0.5


def _mod_parts(mod_ref, mod_row):
    m = mod_ref[pl.ds(mod_row(pl.program_id(0)), 1), :]
    return [m[:, k * D_MODEL:(k + 1) * D_MODEL] for k in range(6)]


def _make_mod_row(is_ctx, seq_len, tile_rows):
    if is_ctx:
        return lambda i: 0
    return lambda i: 1 + (i * tile_rows) // seq_len


def _adaln_kernel(c_ref, w_ref, b_ref, o_ref):
    c = c_ref[...]
    s = (c * jax.nn.sigmoid(c)).astype(BF16)
    o_ref[...] = _dot(s, w_ref[...].astype(BF16)) + b_ref[...]


def _adaln(cond8, w_mod, b_mod):
    depth = w_mod.shape[0]
    n_out = w_mod.shape[2]
    return pl.pallas_call(
        _adaln_kernel,
        grid=(depth, n_out // MOD_TILE),
        in_specs=[
            pl.BlockSpec((MOD_ROWS, D_MODEL), lambda l, n: (0, 0)),
            pl.BlockSpec((None, D_MODEL, MOD_TILE), lambda l, n: (l, 0, n)),
            pl.BlockSpec((None, 1, MOD_TILE), lambda l, n: (l, 0, n)),
        ],
        out_specs=pl.BlockSpec((None, MOD_ROWS, MOD_TILE), lambda l, n: (l, 0, n)),
        out_shape=jax.ShapeDtypeStruct((depth, MOD_ROWS, n_out), F32),
        compiler_params=_cparams(("parallel", "parallel")),
        name="adaln",
    )(cond8, w_mod, b_mod.reshape(depth, 1, n_out))


def _cast_kernel(w_ref, o_ref):
    o_ref[...] = w_ref[...].astype(o_ref.dtype)


def _to_bf16(w):
    nl, a, b = w.shape
    blk = pl.BlockSpec((None, CAST_TILE, CAST_TILE), lambda l, i, k: (l, i, k))
    return pl.pallas_call(
        _cast_kernel,
        grid=(nl, a // CAST_TILE, b // CAST_TILE),
        in_specs=[blk],
        out_specs=blk,
        out_shape=jax.ShapeDtypeStruct(w.shape, BF16),
        compiler_params=_cparams(("parallel", "parallel", "parallel")),
        name="cast_bf16",
    )(w)


def _modulated(x_ref, mod_ref, mod_row):
    sh1, sc1 = _mod_parts(mod_ref, mod_row)[:2]
    return (_rms(x_ref[...]) * (1.0 + sc1) + sh1).astype(BF16)


def _qkv_kernel(x_ref, mod_ref, w_ref, *refs, mod_row, with_cache):
    qkv_refs, wbf_ref = refs[:3], refs[-1]

    @pl.when(pl.program_id(0) == 0)
    def _():
        wbf_ref[...] = w_ref[...].astype(BF16)

    sh1, sc1 = _mod_parts(mod_ref, mod_row)[:2]
    for rb in range(TM_MM // ROW_BLK):
        rows = slice(rb * ROW_BLK, (rb + 1) * ROW_BLK)
        h = (_rms(x_ref[rows, :]) * (1.0 + sc1) + sh1).astype(BF16)
        for g, o_ref in enumerate(qkv_refs):
            o = _dot(h, wbf_ref[:, g * D_MODEL:(g + 1) * D_MODEL])
            o_ref[rows, :] = o.astype(o_ref.dtype)
            if with_cache and g > 0:
                refs[2 + g][rows] = o.reshape(ROW_BLK, NA_HEADS, NA_HEAD_DIM)


def _qkv(x2, mods, layer, mod_row, w, j, with_cache):
    rows = x2.shape[0]
    n_cols = w.shape[2]
    row_spec = pl.BlockSpec((TM_MM, D_MODEL), lambda i: (i, 0))
    out_specs = [row_spec] * 3
    out_shape = [jax.ShapeDtypeStruct((rows, D_MODEL), BF16)] * 3
    if with_cache:
        out_specs += [pl.BlockSpec((TM_MM, NA_HEADS, NA_HEAD_DIM), lambda i: (i, 0, 0))] * 2
        out_shape += [jax.ShapeDtypeStruct((rows, NA_HEADS, NA_HEAD_DIM), F32)] * 2
    return pl.pallas_call(
        functools.partial(_qkv_kernel, mod_row=mod_row, with_cache=with_cache),
        grid=(rows // TM_MM,),
        in_specs=[row_spec,
                  pl.BlockSpec((None, MOD_ROWS, 6 * D_MODEL), lambda i: (layer, 0, 0)),
                  pl.BlockSpec((None, D_MODEL, n_cols), lambda i: (j, 0, 0),
                               pipeline_mode=pl.Buffered(1))],
        out_specs=out_specs,
        out_shape=out_shape,
        scratch_shapes=[pltpu.VMEM((D_MODEL, n_cols), BF16)],
        compiler_params=_cparams(("arbitrary",)),
        name="modmm_qkv",
    )(x2, mods, w)


def _to_time_major(x2):
    return jnp.swapaxes(x2.reshape(SLOTS, PS, CB), 0, 1)


def _from_time_major(x3):
    return jnp.swapaxes(x3, 0, 1).reshape(GROUP_ROWS, CB)


def _slot_iota():
    return lax.broadcasted_iota(jnp.int32, (SLOTS, CB), 0)


def _from_prev_slot(tile):
    return jnp.where(_slot_iota() == 0, 0.0, pltpu.roll(tile, 1, 0))


def _from_next_slot(tile):
    return jnp.where(_slot_iota() == SLOTS - 1, 0.0, pltpu.roll(tile, SLOTS - 1, 0))


def _fill_padded(pad_ref, x, lo, hi, chunked):
    pad_ref[lo:lo + PS] = _to_time_major(x)
    for r in range(lo):
        if chunked:
            pad_ref[r] = _from_prev_slot(pad_ref[PS + r])
        else:
            pad_ref[r] = jnp.zeros((SLOTS, CB), F32)
    for r in range(hi):
        if chunked:
            pad_ref[lo + PS + r] = _from_next_slot(pad_ref[lo + r])
        else:
            pad_ref[lo + PS + r] = jnp.zeros((SLOTS, CB), F32)


def _group_spec():
    return pl.BlockSpec((GROUP_ROWS, CB), lambda g, n: (g, n))


def _chan_spec(lead, j, col0=0):
    return pl.BlockSpec((None, lead, CB), lambda g, n: (j, 0, col0 + n))


def _seq_in_specs(layer):
    return [pl.BlockSpec((GROUP_ROWS, D_MODEL), lambda g, n: (g, 0)),
            pl.BlockSpec((None, MOD_ROWS, 6 * D_MODEL), lambda g, n: (layer, 0, 0))]


def _seq_modulated(x_ref, mod_ref, h_ref, mod_row):
    @pl.when(pl.program_id(1) == 0)
    def _():
        h_ref[...] = _modulated(x_ref, mod_ref, mod_row)


LRU_TC = 128


def _softplus(x):
    return jnp.maximum(x, 0.0) + jnp.log1p(jnp.exp(-jnp.abs(x)))


def _lru_seq_kernel(*refs, chunked, mod_row):
    if chunked:
        (x_ref, mod_ref, wg_ref, wr_ref, cw_ref, cb_ref, wa_ref, ba_ref, wx_ref, bx_ref, lam_ref,
         h0_ref, y_ref, h_ref, gate_ref, pad_ref, af_ref, bf_ref, ab_ref, bb_ref, wbf_ref) = refs
    else:
        (x_ref, mod_ref, wg_ref, wr_ref, cw_ref, cb_ref, wa_ref, ba_ref, wx_ref, bx_ref, lam_ref,
         y_ref, st_ref, h_ref, gate_ref, pad_ref, af_ref, bf_ref, ab_ref, bb_ref, wbf_ref) = refs
    lo = (LRU_CONV_W - 1) // 2
    hi = LRU_CONV_W - 1 - lo
    _seq_modulated(x_ref, mod_ref, h_ref, mod_row)
    h = h_ref[...]
    gate_ref[...] = jax.nn.gelu(_dot(h, wg_ref[...].astype(BF16)), approximate=True)
    _fill_padded(pad_ref, _dot(h, wr_ref[...].astype(BF16)), lo, hi, chunked)

    a_refs = (af_ref, ab_ref)
    b_refs = (bf_ref, bb_ref)
    hc = [0.5 * LRU_C * _softplus(-lam_ref[d:d + 1, :]) for d in range(2)]
    for d in range(2):
        wbf_ref[2 * d] = (0.5 * wa_ref[d]).astype(BF16)
        wbf_ref[2 * d + 1] = (0.5 * wx_ref[d]).astype(BF16)
    hba = [0.5 * ba_ref[d:d + 1, :] for d in range(2)]
    hbx = [0.5 * bx_ref[d:d + 1, :] for d in range(2)]

    def gates(ci, carry):
        t0 = pl.multiple_of(ci * LRU_TC, LRU_TC)
        xf = cb_ref[...] + cw_ref[0:1, :] * pad_ref[pl.ds(t0, LRU_TC)]
        for k in range(1, LRU_CONV_W):
            xf = xf + cw_ref[k:k + 1, :] * pad_ref[pl.ds(t0 + k, LRU_TC)]
        x2 = xf.reshape(LRU_TC * SLOTS, CB)
        xb = x2.astype(BF16)
        hx = 0.5 * x2
        for d in range(2):
            tr = jnp.tanh(_dot(xb, wbf_ref[2 * d]) + hba[d])
            ti = jnp.tanh(_dot(xb, wbf_ref[2 * d + 1]) + hbx[d])
            pos = hc[d] * tr + hc[d]
            a = jnp.exp2(pos * (-LOG2_E))
            one_m_a2 = jnp.tanh(pos) * (a * a + 1.0)
            root = jnp.where(one_m_a2 > 0.0, one_m_a2 * lax.rsqrt(one_m_a2), 0.0)
            bx = root * (hx * ti + hx)
            a_refs[d][pl.ds(t0, LRU_TC)] = a.reshape(LRU_TC, SLOTS, CB)
            b_refs[d][pl.ds(t0, LRU_TC)] = bx.reshape(LRU_TC, SLOTS, CB)
        return carry

    lax.fori_loop(0, PS // LRU_TC, gates, 0)

    zero = jnp.zeros((SLOTS, CB), F32)
    one = jnp.ones((SLOTS, CB), F32)

    def two_steps(a_ref, b_ref, t0, t1, h, p):
        a0, a1 = a_ref[t0], a_ref[t1]
        b0, b1 = b_ref[t0], b_ref[t1]
        a01 = a1 * a0
        b_ref[t0] = a0 * h + b0
        h = a01 * h + (a1 * b0 + b1)
        b_ref[t1] = h
        if chunked:
            a_ref[t0] = a0 * p
            p = a01 * p
            a_ref[t1] = p
        return h, p

    def scan(i, carry):
        hf, hb, pf, pb = carry
        t = 2 * i
        hf, pf = two_steps(af_ref, bf_ref, t, t + 1, hf, pf)
        hb, pb = two_steps(ab_ref, bb_ref, PS - 1 - t, PS - 2 - t, hb, pb)
        return hf, hb, pf, pb

    lax.fori_loop(0, PS // 2, scan, (zero, zero, one, one), unroll=2)

    if chunked:
        slot = _slot_iota()
        h0f = jnp.broadcast_to(h0_ref[0:1, :], (SLOTS, CB))
        h0b = jnp.broadcast_to(h0_ref[1:2, :], (SLOTS, CB))
        end_f, prod_f = bf_ref[PS - 1], af_ref[PS - 1]
        end_b, prod_b = bb_ref[0], ab_ref[0]
        in_f = jnp.where(slot == 0, h0f, 0.0)
        in_b = jnp.where(slot == SLOTS - 1, h0b, 0.0)
        for _ in range(SLOTS - 1):
            in_f = jnp.where(slot == 0, h0f, pltpu.roll(end_f + prod_f * in_f, 1, 0))
            in_b = jnp.where(slot == SLOTS - 1, h0b,
                             pltpu.roll(end_b + prod_b * in_b, SLOTS - 1, 0))
    else:
        st_ref[0] = bf_ref[PS - 1]
        st_ref[1] = bb_ref[0]

    hs = bf_ref[...] + bb_ref[...]
    if chunked:
        hs = hs + af_ref[...] * in_f + ab_ref[...] * in_b
    y_ref[...] = (_from_time_major(hs) * gate_ref[...]).astype(y_ref.dtype)


def _lru_seq(x2, mods, layer, mod_row, p, j, state_lru=None):
    chunked = state_lru is not None
    rows = x2.shape[0]
    groups = rows // GROUP_ROWS
    n_layers = p["lru_conv_b"].shape[0]
    seq = _group_spec()
    wblk = pl.BlockSpec((None, 2, None, CB, CB), lambda g, n: (j, 0, n, 0, 0))
    in_specs = _seq_in_specs(layer) + [
        _chan_spec(D_MODEL, j), _chan_spec(D_MODEL, j, NB),
        _chan_spec(LRU_CONV_W, j), _chan_spec(1, j),
        wblk, _chan_spec(2, j), wblk, _chan_spec(2, j), _chan_spec(2, j)]
    args = [x2, mods] + [p["lru_w_in"]] * 2 + [
            p["lru_conv_w"], p["lru_conv_b"].reshape(n_layers, 1, D_MODEL),
            p["lru_w_a"], p["lru_b_a"], p["lru_w_x"], p["lru_b_x"], p["lru_lambda"]]
    y_shape = jax.ShapeDtypeStruct((rows, D_MODEL), BF16)
    if chunked:
        in_specs.append(pl.BlockSpec((None, None, 2, CB), lambda g, n: (g, j, 0, n)))
        args.append(state_lru)
        out_specs = [seq]
        out_shape = [y_shape]
    else:
        out_specs = [seq, pl.BlockSpec((2, SLOTS, CB), lambda g, n: (0, g, n))]
        out_shape = [y_shape, jax.ShapeDtypeStruct((2, groups * SLOTS, D_MODEL), F32)]
    tile = (PS, SLOTS, CB)
    outs = pl.pallas_call(
        functools.partial(_lru_seq_kernel, chunked=chunked, mod_row=mod_row),
        grid=(groups, NB),
        in_specs=in_specs,
        out_specs=out_specs,
        out_shape=out_shape,
        scratch_shapes=[pltpu.VMEM((GROUP_ROWS, D_MODEL), BF16), pltpu.VMEM((GROUP_ROWS, CB), F32),
                        pltpu.VMEM((PS + LRU_CONV_W - 1, SLOTS, CB), F32)]
        + [pltpu.VMEM(tile, F32) for _ in range(4)] + [pltpu.VMEM((4, CB, CB), BF16)],
        compiler_params=_cparams(("parallel", "arbitrary")),
        name="lru_seq_dec" if chunked else "lru_seq_ctx",
    )(*args)
    return (outs[0], None) if chunked else (outs[0], outs[1])


CONV_TC = 128


def _conv_seq_kernel(x_ref, mod_ref, wv_ref, wg_ref, bv_ref, bg_ref, w_ref, b_ref, o_ref,
                     h_ref, pad_ref, out_ref, *, chunked, mod_row):
    lo = (CONF_CONV_W - 1) // 2
    hi = CONF_CONV_W - 1 - lo
    _seq_modulated(x_ref, mod_ref, h_ref, mod_row)
    h = h_ref[...]
    val = _dot(h, wv_ref[...].astype(BF16)) + bv_ref[...]
    gate = _dot(h, wg_ref[...].astype(BF16)) + bg_ref[...]
    _fill_padded(pad_ref, val * _sigmoid(gate), lo, hi, chunked)

    def chunk(ci, carry):
        t0 = pl.multiple_of(ci * CONV_TC, CONV_TC)
        acc = b_ref[...] + w_ref[0:1, :] * pad_ref[pl.ds(t0, CONV_TC)]
        for k in range(1, CONF_CONV_W):
            acc = acc + w_ref[k:k + 1, :] * pad_ref[pl.ds(t0 + k, CONV_TC)]
        out_ref[pl.ds(t0, CONV_TC)] = acc
        return carry

    lax.fori_loop(0, PS // CONV_TC, chunk, 0)
    o_ref[...] = _from_time_major(out_ref[...])


def _conv_seq(x2, mods, layer, mod_row, p, j, chunked):
    rows = x2.shape[0]
    n_layers = p["conf_dw_b"].shape[0]
    w_pw1 = p["conf_w_pw1"]
    b_pw1 = p["conf_b_pw1"].reshape(n_layers, 1, 2 * D_MODEL)
    return pl.pallas_call(
        functools.partial(_conv_seq_kernel, chunked=chunked, mod_row=mod_row),
        grid=(rows // GROUP_ROWS, NB),
        in_specs=_seq_in_specs(layer) + [
            _chan_spec(D_MODEL, j), _chan_spec(D_MODEL, j, NB),
            _chan_spec(1, j), _chan_spec(1, j, NB),
            _chan_spec(CONF_CONV_W, j), _chan_spec(1, j)],
        out_specs=_group_spec(),
        out_shape=jax.ShapeDtypeStruct((rows, D_MODEL), F32),
        scratch_shapes=[pltpu.VMEM((GROUP_ROWS, D_MODEL), BF16),
                        pltpu.VMEM((PS + CONF_CONV_W - 1, SLOTS, CB), F32),
                        pltpu.VMEM((PS, SLOTS, CB), F32)],
        compiler_params=_cparams(("parallel", "arbitrary")),
        name="conv_seq_dec" if chunked else "conv_seq_ctx",
    )(x2, mods, w_pw1, w_pw1, b_pw1, b_pw1, p["conf_dw_w"],
      p["conf_dw_b"].reshape(n_layers, 1, D_MODEL))


def _attend(q, parts):
    lane = lax.broadcasted_iota(jnp.int32, (1, LANES), 1)
    m = q.shape[0]
    zero = jnp.zeros_like(q)
    qm = jnp.concatenate([jnp.where(lane < NA_HEAD_DIM, q, zero),
                          jnp.where(lane >= NA_HEAD_DIM, q, zero)], axis=0) * ATT_SCALE
    scores = []
    for k, _, bias in parts:
        s = _dot_t(qm, k)
        if bias is not None:
            ref, head = bias
            s = s + jnp.concatenate([ref[head], ref[head + 1]], axis=0)
        scores.append(s)
    mx = scores[0].max(axis=-1, keepdims=True)
    for s in scores[1:]:
        mx = jnp.maximum(mx, s.max(axis=-1, keepdims=True))
    den = 0.0
    acc = 0.0
    for s, (_, v, _) in zip(scores, parts):
        pr = jnp.exp(s - mx)
        den = den + pr.sum(axis=-1, keepdims=True)
        acc = acc + _dot(pr.astype(BF16), v)
    o = acc / den
    return jnp.where(lane < NA_HEAD_DIM, o[:m], o[m:])


def _attn_ctx_kernel(q_ref, k_ref, v_ref, o_ref):
    for s in range(D_MODEL // LANES):
        sl = slice(s * LANES, (s + 1) * LANES)
        k = k_ref[:, sl].astype(BF16)
        v = v_ref[:, sl].astype(BF16)
        o_ref[:, sl] = _attend(q_ref[:, sl], [(k, v, None)]).astype(o_ref.dtype)


def _attn_ctx(q, k, v, batch):
    blk = pl.BlockSpec((PS, D_MODEL), lambda b: (b, 0))
    return pl.pallas_call(
        _attn_ctx_kernel,
        grid=(batch,),
        in_specs=[blk, blk, blk],
        out_specs=blk,
        out_shape=jax.ShapeDtypeStruct(q.shape, BF16),
        compiler_params=_cparams(("parallel",)),
        name="attn_ctx",
    )(q, k, v)


def _attn_dec_kernel(q_ref, k_ref, v_ref, kc_ref, vc_ref, bias_ref, o_ref):
    i = pl.program_id(0)
    row0 = jnp.clip(Q_ROWS * i - NA_WIN_ROWS // 2, 0, GRID_H - WIN_ROWS_BLK)
    win = pl.ds(pl.multiple_of(row0 * GRID_W, GRID_W), WIN_KEYS)
    for s in range(DEC_SLABS):
        sl = slice(s * LANES, (s + 1) * LANES)
        kc = kc_ref[:, sl].astype(BF16)
        vc = vc_ref[:, sl].astype(BF16)
        o = _attend(q_ref[:, sl], [(k_ref[win, sl], v_ref[win, sl], (bias_ref, 2 * s)),
                                   (kc, vc, None)])
        o_ref[:, sl] = o.astype(o_ref.dtype)


def _qblk_window(i):
    row0 = min(max(Q_ROWS * i - NA_WIN_ROWS // 2, 0), GRID_H - WIN_ROWS_BLK)
    out = []
    for a in range(Q_ROWS):
        r = Q_ROWS * i + a
        rs = min(max(r - NA_WIN_ROWS // 2, 0), GRID_H - NA_WIN_ROWS)
        out.append((r, [rs <= row0 + w < rs + NA_WIN_ROWS for w in range(WIN_ROWS_BLK)]))
    return row0, out


BIAS_CLASSES = (0, 1, N_QBLK - 1)


def _bias_kernel(rpb_ref, o_ref):
    c = lax.broadcasted_iota(jnp.int32, (GRID_W, LANES), 0)
    l = lax.broadcasted_iota(jnp.int32, (GRID_W, LANES), 1)
    cs = jnp.clip(c - NA_WIN_COLS // 2, 0, GRID_W - NA_WIN_COLS)
    in_cols = (l >= cs) & (l < cs + NA_WIN_COLS)
    neg = jnp.full((GRID_W, LANES), NEG_BIG, F32)
    lo_half, hi_half = [], []
    for dr in range(2 * NA_WIN_ROWS - 1):
        row = jnp.broadcast_to(rpb_ref[dr:dr + 1, :], (GRID_W, LANES))
        t = pltpu.roll(row, LANES - (NA_WIN_COLS - 1), 1, stride=1, stride_axis=0)
        t = jnp.where(in_cols, t, NEG_BIG)
        lo_half.append(t)
        hi_half.append(pltpu.roll(t, GRID_W, 1))
    for cls, i in enumerate(BIAS_CLASSES):
        row0, qrows = _qblk_window(i)
        for a, (r, valid) in enumerate(qrows):
            for wp in range(WIN_ROWS_BLK // 2):
                halves = []
                for half, bank in enumerate((lo_half, hi_half)):
                    w = 2 * wp + half
                    halves.append(bank[row0 + w - r + NA_WIN_ROWS - 1] if valid[w] else neg)
                o_ref[cls, a * GRID_W:(a + 1) * GRID_W, wp * LANES:(wp + 1) * LANES] = (
                    jnp.where(l < GRID_W, halves[0], halves[1]))


def _attn_bias_table(rpb, j):
    nl, nh, ndr, ndc = rpb.shape
    ndr_pad = -(-ndr // SUBLANES) * SUBLANES
    rpb_p = jnp.pad(rpb, ((0, 0), (0, 0), (0, ndr_pad - ndr), (0, LANES - ndc)))
    return pl.pallas_call(
        _bias_kernel,
        grid=(nh,),
        in_specs=[pl.BlockSpec((None, None, ndr_pad, LANES), lambda h: (j, h, 0, 0))],
        out_specs=pl.BlockSpec((len(BIAS_CLASSES), None, Q_BLK, WIN_KEYS), lambda h: (0, h, 0, 0)),
        out_shape=jax.ShapeDtypeStruct((len(BIAS_CLASSES), nh, Q_BLK, WIN_KEYS), F32),
        compiler_params=_cparams(("parallel",)),
        name="attn_bias",
    )(rpb_p)


def _attn_dec(q, k, v, kc, vc, j, bias, batch):
    t = GRID_H * GRID_W
    assert DEC_SLABS * LANES == D_MODEL
    qblk = pl.BlockSpec((Q_BLK, D_MODEL), lambda i, b: (b * N_QBLK + i, 0))
    kvblk = pl.BlockSpec((None, t, D_MODEL), lambda i, b: (b, 0, 0))
    cblk = pl.BlockSpec((None, None, PS, D_MODEL), lambda i, b: (b, j, 0, 0))

    def bias_idx(i, b):
        cls = (i > 0).astype(jnp.int32) + (i == N_QBLK - 1).astype(jnp.int32)
        return (cls, 0, 0, 0)

    return pl.pallas_call(
        _attn_dec_kernel,
        grid=(N_QBLK, batch),
        in_specs=[qblk, kvblk, kvblk, cblk, cblk,
                  pl.BlockSpec((None, NA_HEADS, Q_BLK, WIN_KEYS), bias_idx,
                               pipeline_mode=pl.Buffered(1))],
        out_specs=qblk,
        out_shape=jax.ShapeDtypeStruct(q.shape, BF16),
        compiler_params=_cparams(("arbitrary", "arbitrary")),
        name="attn_dec",
    )(q, k.reshape(batch, t, D_MODEL), v.reshape(batch, t, D_MODEL), kc, vc, bias)


def _pffn_kernel(*refs, conf, final, mod_row):
    refs = list(refs)
    x_ref, y_ref, mod_ref, wp_ref = refs[:4]
    pos = 4
    bp_ref = lng_ref = lnb_ref = fin_ref = None
    if conf:
        bp_ref, lng_ref, lnb_ref = refs[pos:pos + 3]
        pos += 3
    w1_ref, w2_ref = refs[pos:pos + 2]
    pos += 2
    if final:
        fin_ref = refs[pos]
        pos += 1
    o_ref, h2_ref, wpb_ref = refs[pos:pos + 3]
    f = pl.program_id(1)

    @pl.when(f == 0)
    def _():
        _, _, g1, sh2, sc2, _ = _mod_parts(mod_ref, mod_row)
        wpb_ref[...] = wp_ref[...].astype(BF16)
        for rb in range(TM // ROW_BLK):
            rows = slice(rb * ROW_BLK, (rb + 1) * ROW_BLK)
            if conf:
                z = y_ref[rows, :]
                mu = jnp.mean(z, axis=-1, keepdims=True)
                zc = z - mu
                var = jnp.mean(zc * zc, axis=-1, keepdims=True)
                zn = zc * lax.rsqrt(var + EPS) * lng_ref[...] + lnb_ref[...]
                y = (zn * _sigmoid(zn)).astype(BF16)
            else:
                y = y_ref[rows, :]
            proj = _dot(y, wpb_ref[...])
            if bp_ref is not None:
                proj = proj + bp_ref[...]
            x1 = x_ref[rows, :] + g1 * proj
            o_ref[rows, :] = x1
            h2_ref[rows, :] = (_rms(x1) * (1.0 + sc2) + sh2).astype(BF16)

    u = _dot(h2_ref[...], w1_ref[...])
    u = jnp.square(jnp.maximum(u, 0.0)).astype(BF16)
    g2 = _mod_parts(mod_ref, mod_row)[5]
    o_ref[...] += g2 * _dot(u, w2_ref[...])

    if final:
        @pl.when(f == pl.num_programs(1) - 1)
        def _():
            o_ref[...] = _rms(o_ref[...]) * fin_ref[...]


def _pffn(x2, y, mods, layer, mod_row, w_proj, j, w1, w2, b_proj=None, ln_g=None, ln_b=None,
          final_g=None):
    rows = x2.shape[0]
    conf = ln_g is not None
    final = final_g is not None
    row_spec = pl.BlockSpec((TM, D_MODEL), lambda i, f: (i, 0))
    vec_spec = pl.BlockSpec((None, 1, D_MODEL), lambda i, f: (j, 0, 0))
    in_specs = [row_spec, row_spec,
                pl.BlockSpec((None, MOD_ROWS, 6 * D_MODEL), lambda i, f: (layer, 0, 0)),
                pl.BlockSpec((None, D_MODEL, D_MODEL), lambda i, f: (j, 0, 0),
                             pipeline_mode=pl.Buffered(1))]
    args = [x2, y, mods, w_proj]
    if conf:
        in_specs += [vec_spec, vec_spec, vec_spec]
        args += [v.reshape(v.shape[0], 1, D_MODEL) for v in (b_proj, ln_g, ln_b)]
    fk = FK_CONF if conf else FK
    in_specs += [pl.BlockSpec((None, D_MODEL, fk), lambda i, f: (layer, 0, f)),
                 pl.BlockSpec((None, fk, D_MODEL), lambda i, f: (layer, f, 0))]
    args += [w1, w2]
    if final:
        in_specs.append(pl.BlockSpec((1, D_MODEL), lambda i, f: (0, 0)))
        args.append(final_g.reshape(1, D_MODEL))
    return pl.pallas_call(
        functools.partial(_pffn_kernel, conf=conf, final=final, mod_row=mod_row),
        grid=(rows // TM, D_FF // fk),
        in_specs=in_specs,
        out_specs=row_spec,
        out_shape=jax.ShapeDtypeStruct((rows, D_MODEL), F32),
        scratch_shapes=[pltpu.VMEM((TM, D_MODEL), BF16), pltpu.VMEM((D_MODEL, D_MODEL), BF16)],
        compiler_params=_cparams(("parallel", "arbitrary")),
        name="pffn" + ("_conf" if conf else "") + ("_final" if final else ""),
    )(*args)


def _trunk(x, mods, is_ctx, p, state_lru, cache_k, cache_v, bias_tabs):
    bsz, t, d = x.shape
    rows = bsz * t
    assert d == D_MODEL and rows % GROUP_ROWS == 0 and rows % TM == 0
    if is_ctx:
        assert t == PS
    else:
        assert t == GROUP_ROWS == GRID_H * GRID_W and cache_k.shape[2] == PS
    x2 = x.reshape(rows, d)
    depth = mods.shape[0]
    row_mm = _make_mod_row(is_ctx, t, TM_MM)
    row_ffn = _make_mod_row(is_ctx, t, TM)
    row_seq = _make_mod_row(is_ctx, t, GROUP_ROWS)
    states, ks, vs = [], [], []
    for i in range(depth):
        kind, j = i % 3, i // 3
        fin = p["final_g"] if i == depth - 1 else None
        ffn = functools.partial(_pffn, x2, mods=mods, layer=i, mod_row=row_ffn, j=j,
                                w1=p["w_ff1"], w2=p["w_ff2"], final_g=fin)
        if kind == 0:
            y, st = _lru_seq(x2, mods, i, row_seq, p, j, None if is_ctx else state_lru)
            if is_ctx:
                states.append(st)
            x2 = ffn(y=y, w_proj=p["lru_w_out"])
        elif kind == 1:
            zc = _conv_seq(x2, mods, i, row_seq, p, j, chunked=not is_ctx)
            x2 = ffn(y=zc, w_proj=p["conf_w_pw2"], b_proj=p["conf_b_pw2"],
                     ln_g=p["conf_ln_g"], ln_b=p["conf_ln_b"])
        else:
            if is_ctx:
                q, k, v, k_cache, v_cache = _qkv(x2, mods, i, row_mm, p["na_w_qkv"], j, True)
                o = _attn_ctx(q, k, v, bsz)
                ks.append(k_cache.reshape(bsz, t, NA_HEADS, NA_HEAD_DIM))
                vs.append(v_cache.reshape(bsz, t, NA_HEADS, NA_HEAD_DIM))
            else:
                q, k, v = _qkv(x2, mods, i, row_mm, p["na_w_qkv"], j, False)
                kc = cache_k.reshape(cache_k.shape[:3] + (d,))
                vc = cache_v.reshape(cache_v.shape[:3] + (d,))
                o = _attn_dec(q, k, v, kc, vc, j, bias_tabs[j], bsz)
            x2 = ffn(y=o, w_proj=p["na_w_o"])
    return x2.reshape(bsz, t, d), states, ks, vs


def kernel(x_prompt, x_sample, state_lru, cache_k, cache_v, c, c_ctx, w_mod, b_mod, w_ff1, w_ff2, lru_w_in, lru_conv_w, lru_conv_b, lru_w_a, lru_b_a, lru_w_x, lru_b_x, lru_lambda, lru_w_out, conf_w_pw1, conf_b_pw1, conf_dw_w, conf_dw_b, conf_ln_g, conf_ln_b, conf_w_pw2, conf_b_pw2, na_w_qkv, na_w_o, na_rpb, final_g):
    p = dict(w_ff1=_to_bf16(w_ff1), w_ff2=_to_bf16(w_ff2), lru_w_in=lru_w_in, lru_conv_w=lru_conv_w,
             lru_conv_b=lru_conv_b, lru_w_a=lru_w_a, lru_b_a=lru_b_a, lru_w_x=lru_w_x,
             lru_b_x=lru_b_x, lru_lambda=lru_lambda, lru_w_out=lru_w_out,
             conf_w_pw1=conf_w_pw1, conf_b_pw1=conf_b_pw1, conf_dw_w=conf_dw_w,
             conf_dw_b=conf_dw_b, conf_ln_g=conf_ln_g, conf_ln_b=conf_ln_b,
             conf_w_pw2=conf_w_pw2, conf_b_pw2=conf_b_pw2, na_w_qkv=na_w_qkv, na_w_o=na_w_o,
             final_g=final_g)
    dec_b = c.shape[0]
    assert 1 + dec_b <= MOD_ROWS
    cond8 = jnp.concatenate(
        [c_ctx[None, :], c, jnp.zeros((MOD_ROWS - 1 - dec_b, D_MODEL), F32)], axis=0)
    mods = _adaln(cond8, w_mod, b_mod)
    bias_tabs = [_attn_bias_table(na_rpb, j) for j in range(na_rpb.shape[0])]

    y_prompt, states, ks, vs = _trunk(x_prompt, mods, True, p, None, None, None, None)
    y_sample, _, _, _ = _trunk(x_sample, mods, False, p, state_lru, cache_k, cache_v, bias_tabs)

    new_state = jnp.stack([jnp.transpose(s, (1, 0, 2)) for s in states], axis=1)
    new_k = jnp.stack(ks, axis=1)
    new_v = jnp.stack(vs, axis=1)
    return (y_prompt, y_sample, new_state, new_k, new_v)
```

```python
import functools
import math

import jax
import jax.numpy as jnp
from jax import lax
from jax.experimental import pallas as pl
from jax.experimental.pallas import tpu as pltpu

F32 = jnp.float32
BF16 = jnp.bfloat16

D_MODEL = 1024
D_FF = 4 * D_MODEL
PS = 256
CB = 256
NB = D_MODEL // CB
SLOTS = 8
GROUP_ROWS = SLOTS * PS
GRID_W = 64
GRID_H = 32
NA_HEADS = 16
NA_HEAD_DIM = 64
NA_WIN_ROWS = 8
NA_WIN_COLS = 16
ATT_SCALE = NA_HEAD_DIM ** -0.5
assert math.frexp(ATT_SCALE)[0] == 0.5, "the attention kernels scale bf16 queries exactly"
LRU_C = 8.0
LRU_CONV_W = 4
LOG2_E = math.log2(math.e)
CONF_CONV_W = 31
EPS = 1e-6
NEG_BIG = -1e30

LANES = 128
SUBLANES = 8
MOD_ROWS = SUBLANES
TM = 1024
TM_FULL = 512
TM_MM = 512
ROW_BLK = 256
FK_CONF = 1024
CAST_TILE = 1024
MOD_TILE = 1536
Q_ROWS = 4
Q_BLK = Q_ROWS * GRID_W
N_QBLK = GRID_H // Q_ROWS
WIN_ROWS_BLK = 12
WIN_KEYS = WIN_ROWS_BLK * GRID_W
DEC_SLABS = 8
VMEM_LIMIT = 56 * 1024 * 1024


def _cparams(sem):
    return pltpu.CompilerParams(dimension_semantics=sem, vmem_limit_bytes=VMEM_LIMIT)


def _dot(a, b):
    return jnp.dot(a, b, preferred_element_type=F32)


def _dot_t(a, b):
    return lax.dot_general(a, b, (((1,), (1,)), ((), ())), preferred_element_type=F32)


def _rms(x):
    return x * lax.rsqrt(jnp.mean(x * x, axis=-1, keepdims=True) + EPS)


def _sigmoid(x):
    return 0.5 * jnp.tanh(0.5 * x) + 0.5


def _mod_parts(mod_ref, mod_row):
    m = mod_ref[pl.ds(mod_row(pl.program_id(0)), 1), :]
    return [m[:, k * D_MODEL:(k + 1) * D_MODEL] for k in range(6)]


def _make_mod_row(is_ctx, seq_len, tile_rows):
    if is_ctx:
        return lambda i: 0
    return lambda i: 1 + (i * tile_rows) // seq_len


def _adaln_kernel(c_ref, w_ref, b_ref, o_ref):
    c = c_ref[...]
    s = (c * jax.nn.sigmoid(c)).astype(BF16)
    o_ref[...] = _dot(s, w_ref[...].astype(BF16)) + b_ref[...]


def _adaln(cond8, w_mod, b_mod):
    depth = w_mod.shape[0]
    n_out = w_mod.shape[2]
    return pl.pallas_call(
        _adaln_kernel,
        grid=(depth, n_out // MOD_TILE),
        in_specs=[
            pl.BlockSpec((MOD_ROWS, D_MODEL), lambda l, n: (0, 0)),
            pl.BlockSpec((None, D_MODEL, MOD_TILE), lambda l, n: (l, 0, n)),
            pl.BlockSpec((None, 1, MOD_TILE), lambda l, n: (l, 0, n)),
        ],
        out_specs=pl.BlockSpec((None, MOD_ROWS, MOD_TILE), lambda l, n: (l, 0, n)),
        out_shape=jax.ShapeDtypeStruct((depth, MOD_ROWS, n_out), F32),
        compiler_params=_cparams(("parallel", "parallel")),
        name="adaln",
    )(cond8, w_mod, b_mod.reshape(depth, 1, n_out))


def _cast_kernel(w_ref, o_ref):
    o_ref[...] = w_ref[...].astype(o_ref.dtype)


def _to_bf16(w):
    nl, a, b = w.shape
    blk = pl.BlockSpec((None, CAST_TILE, CAST_TILE), lambda l, i, k: (l, i, k))
    return pl.pallas_call(
        _cast_kernel,
        grid=(nl, a // CAST_TILE, b // CAST_TILE),
        in_specs=[blk],
        out_specs=blk,
        out_shape=jax.ShapeDtypeStruct(w.shape, BF16),
        compiler_params=_cparams(("parallel", "parallel", "parallel")),
        name="cast_bf16",
    )(w)


def _modulated(x_ref, mod_ref, mod_row):
    sh1, sc1 = _mod_parts(mod_ref, mod_row)[:2]
    return (_rms(x_ref[...]) * (1.0 + sc1) + sh1).astype(BF16)


def _qkv_kernel(x_ref, mod_ref, w_ref, *refs, mod_row, with_cache):
    qkv_refs, wbf_ref = refs[:3], refs[-1]

    @pl.when(pl.program_id(0) == 0)
    def _():
        wbf_ref[...] = w_ref[...].astype(BF16)

    sh1, sc1 = _mod_parts(mod_ref, mod_row)[:2]
    for rb in range(TM_MM // ROW_BLK):
        rows = slice(rb * ROW_BLK, (rb + 1) * ROW_BLK)
        h = (_rms(x_ref[rows, :]) * (1.0 + sc1) + sh1).astype(BF16)
        for g, o_ref in enumerate(qkv_refs):
            o = _dot(h, wbf_ref[:, g * D_MODEL:(g + 1) * D_MODEL])
            o_ref[rows, :] = o.astype(o_ref.dtype)
            if with_cache and g > 0:
                refs[2 + g][rows] = o.reshape(ROW_BLK, NA_HEADS, NA_HEAD_DIM)


def _qkv(x2, mods, layer, mod_row, w, j, with_cache):
    rows = x2.shape[0]
    n_cols = w.shape[2]
    row_spec = pl.BlockSpec((TM_MM, D_MODEL), lambda i: (i, 0))
    out_specs = [row_spec] * 3
    out_shape = [jax.ShapeDtypeStruct((rows, D_MODEL), BF16)] * 3
    if with_cache:
        out_specs += [pl.BlockSpec((TM_MM, NA_HEADS, NA_HEAD_DIM), lambda i: (i, 0, 0))] * 2
        out_shape += [jax.ShapeDtypeStruct((rows, NA_HEADS, NA_HEAD_DIM), F32)] * 2
    return pl.pallas_call(
        functools.partial(_qkv_kernel, mod_row=mod_row, with_cache=with_cache),
        grid=(rows // TM_MM,),
        in_specs=[row_spec,
                  pl.BlockSpec((None, MOD_ROWS, 6 * D_MODEL), lambda i: (layer, 0, 0)),
                  pl.BlockSpec((None, D_MODEL, n_cols), lambda i: (j, 0, 0),
                               pipeline_mode=pl.Buffered(1))],
        out_specs=out_specs,
        out_shape=out_shape,
        scratch_shapes=[pltpu.VMEM((D_MODEL, n_cols), BF16)],
        compiler_params=_cparams(("arbitrary",)),
        name="modmm_qkv",
    )(x2, mods, w)


def _to_time_major(x2):
    return jnp.swapaxes(x2.reshape(SLOTS, PS, CB), 0, 1)


def _from_time_major(x3):
    return jnp.swapaxes(x3, 0, 1).reshape(GROUP_ROWS, CB)


def _slot_iota():
    return lax.broadcasted_iota(jnp.int32, (SLOTS, CB), 0)


def _from_prev_slot(tile):
    return jnp.where(_slot_iota() == 0, 0.0, pltpu.roll(tile, 1, 0))


def _from_next_slot(tile):
    return jnp.where(_slot_iota() == SLOTS - 1, 0.0, pltpu.roll(tile, SLOTS - 1, 0))


def _fill_padded(pad_ref, x, lo, hi, chunked):
    pad_ref[lo:lo + PS] = _to_time_major(x)
    for r in range(lo):
        if chunked:
            pad_ref[r] = _from_prev_slot(pad_ref[PS + r])
        else:
            pad_ref[r] = jnp.zeros((SLOTS, CB), F32)
    for r in range(hi):
        if chunked:
            pad_ref[lo + PS + r] = _from_next_slot(pad_ref[lo + r])
        else:
            pad_ref[lo + PS + r] = jnp.zeros((SLOTS, CB), F32)


def _group_spec():
    return pl.BlockSpec((GROUP_ROWS, CB), lambda g, n: (g, n))


def _chan_spec(lead, j, col0=0):
    return pl.BlockSpec((None, lead, CB), lambda g, n: (j, 0, col0 + n))


def _seq_in_specs(layer):
    return [pl.BlockSpec((GROUP_ROWS, D_MODEL), lambda g, n: (g, 0)),
            pl.BlockSpec((None, MOD_ROWS, 6 * D_MODEL), lambda g, n: (layer, 0, 0))]


def _seq_modulated(x_ref, mod_ref, h_ref, mod_row):
    @pl.when(pl.program_id(1) == 0)
    def _():
        h_ref[...] = _modulated(x_ref, mod_ref, mod_row)


LRU_TC = 128


def _softplus(x):
    return jnp.maximum(x, 0.0) + jnp.log1p(jnp.exp(-jnp.abs(x)))


def _lru_seq_kernel(*refs, chunked, mod_row):
    if chunked:
        (x_ref, mod_ref, wg_ref, wr_ref, cw_ref, cb_ref, wa_ref, ba_ref, wx_ref, bx_ref, lam_ref,
         h0_ref, y_ref, h_ref, gate_ref, pad_ref, af_ref, bf_ref, ab_ref, bb_ref, wbf_ref) = refs
    else:
        (x_ref, mod_ref, wg_ref, wr_ref, cw_ref, cb_ref, wa_ref, ba_ref, wx_ref, bx_ref, lam_ref,
         y_ref, st_ref, h_ref, gate_ref, pad_ref, af_ref, bf_ref, ab_ref, bb_ref, wbf_ref) = refs
    lo = (LRU_CONV_W - 1) // 2
    hi = LRU_CONV_W - 1 - lo
    _seq_modulated(x_ref, mod_ref, h_ref, mod_row)
    h = h_ref[...]
    gate_ref[...] = jax.nn.gelu(_dot(h, wg_ref[...].astype(BF16)), approximate=True)
    _fill_padded(pad_ref, _dot(h, wr_ref[...].astype(BF16)), lo, hi, chunked)

    a_refs = (af_ref, ab_ref)
    b_refs = (bf_ref, bb_ref)
    hc = [0.5 * LRU_C * _softplus(-lam_ref[d:d + 1, :]) for d in range(2)]
    for d in range(2):
        wbf_ref[2 * d] = (0.5 * wa_ref[d]).astype(BF16)
        wbf_ref[2 * d + 1] = (0.5 * wx_ref[d]).astype(BF16)
    hba = [0.5 * ba_ref[d:d + 1, :] for d in range(2)]
    hbx = [0.5 * bx_ref[d:d + 1, :] for d in range(2)]

    def gates(ci, carry):
        t0 = pl.multiple_of(ci * LRU_TC, LRU_TC)
        xf = cb_ref[...] + cw_ref[0:1, :] * pad_ref[pl.ds(t0, LRU_TC)]
        for k in range(1, LRU_CONV_W):
            xf = xf + cw_ref[k:k + 1, :] * pad_ref[pl.ds(t0 + k, LRU_TC)]
        x2 = xf.reshape(LRU_TC * SLOTS, CB)
        xb = x2.astype(BF16)
        hx = 0.5 * x2
        for d in range(2):
            tr = jnp.tanh(_dot(xb, wbf_ref[2 * d]) + hba[d])
            ti = jnp.tanh(_dot(xb, wbf_ref[2 * d + 1]) + hbx[d])
            pos = hc[d] * tr + hc[d]
            a = jnp.exp2(pos * (-LOG2_E))
            one_m_a2 = jnp.tanh(pos) * (a * a + 1.0)
            root = jnp.where(one_m_a2 > 0.0, one_m_a2 * lax.rsqrt(one_m_a2), 0.0)
            bx = root * (hx * ti + hx)
            a_refs[d][pl.ds(t0, LRU_TC)] = a.reshape(LRU_TC, SLOTS, CB)
            b_refs[d][pl.ds(t0, LRU_TC)] = bx.reshape(LRU_TC, SLOTS, CB)
        return carry

    lax.fori_loop(0, PS // LRU_TC, gates, 0)

    zero = jnp.zeros((SLOTS, CB), F32)
    one = jnp.ones((SLOTS, CB), F32)

    def two_steps(a_ref, b_ref, t0, t1, h, p):
        a0, a1 = a_ref[t0], a_ref[t1]
        b0, b1 = b_ref[t0], b_ref[t1]
        a01 = a1 * a0
        b_ref[t0] = a0 * h + b0
        h = a01 * h + (a1 * b0 + b1)
        b_ref[t1] = h
        if chunked:
            a_ref[t0] = a0 * p
            p = a01 * p
            a_ref[t1] = p
        return h, p

    def scan(i, carry):
        hf, hb, pf, pb = carry
        t = 2 * i
        hf, pf = two_steps(af_ref, bf_ref, t, t + 1, hf, pf)
        hb, pb = two_steps(ab_ref, bb_ref, PS - 1 - t, PS - 2 - t, hb, pb)
        return hf, hb, pf, pb

    lax.fori_loop(0, PS // 2, scan, (zero, zero, one, one), unroll=2)

    if chunked:
        slot = _slot_iota()
        h0f = jnp.broadcast_to(h0_ref[0:1, :], (SLOTS, CB))
        h0b = jnp.broadcast_to(h0_ref[1:2, :], (SLOTS, CB))
        end_f, prod_f = bf_ref[PS - 1], af_ref[PS - 1]
        end_b, prod_b = bb_ref[0], ab_ref[0]
        in_f = jnp.where(slot == 0, h0f, 0.0)
        in_b = jnp.where(slot == SLOTS - 1, h0b, 0.0)
        for _ in range(SLOTS - 1):
            in_f = jnp.where(slot == 0, h0f, pltpu.roll(end_f + prod_f * in_f, 1, 0))
            in_b = jnp.where(slot == SLOTS - 1, h0b,
                             pltpu.roll(end_b + prod_b * in_b, SLOTS - 1, 0))
    else:
        st_ref[0] = bf_ref[PS - 1]
        st_ref[1] = bb_ref[0]

    def combine(ci, carry):
        sl = pl.ds(pl.multiple_of(ci * LRU_TC, LRU_TC), LRU_TC)
        hs = bf_ref[sl] + bb_ref[sl]
        if chunked:
            hs = hs + af_ref[sl] * in_f + ab_ref[sl] * in_b
        bf_ref[sl] = hs
        return carry

    lax.fori_loop(0, PS // LRU_TC, combine, 0)
    y_ref[...] = (_from_time_major(bf_ref[...]) * gate_ref[...]).astype(y_ref.dtype)


def _lru_seq(x2, mods, layer, mod_row, p, j, state_lru=None):
    chunked = state_lru is not None
    rows = x2.shape[0]
    groups = rows // GROUP_ROWS
    n_layers = p["lru_conv_b"].shape[0]
    seq = _group_spec()
    wblk = pl.BlockSpec((None, 2, None, CB, CB), lambda g, n: (j, 0, n, 0, 0))
    in_specs = _seq_in_specs(layer) + [
        _chan_spec(D_MODEL, j), _chan_spec(D_MODEL, j, NB),
        _chan_spec(LRU_CONV_W, j), _chan_spec(1, j),
        wblk, _chan_spec(2, j), wblk, _chan_spec(2, j), _chan_spec(2, j)]
    args = [x2, mods] + [p["lru_w_in"]] * 2 + [
            p["lru_conv_w"], p["lru_conv_b"].reshape(n_layers, 1, D_MODEL),
            p["lru_w_a"], p["lru_b_a"], p["lru_w_x"], p["lru_b_x"], p["lru_lambda"]]
    y_shape = jax.ShapeDtypeStruct((rows, D_MODEL), BF16)
    if chunked:
        in_specs.append(pl.BlockSpec((None, None, 2, CB), lambda g, n: (g, j, 0, n)))
        args.append(state_lru)
        out_specs = [seq]
        out_shape = [y_shape]
    else:
        out_specs = [seq, pl.BlockSpec((2, SLOTS, CB), lambda g, n: (0, g, n))]
        out_shape = [y_shape, jax.ShapeDtypeStruct((2, groups * SLOTS, D_MODEL), F32)]
    tile = (PS, SLOTS, CB)
    outs = pl.pallas_call(
        functools.partial(_lru_seq_kernel, chunked=chunked, mod_row=mod_row),
        grid=(groups, NB),
        in_specs=in_specs,
        out_specs=out_specs,
        out_shape=out_shape,
        scratch_shapes=[pltpu.VMEM((GROUP_ROWS, D_MODEL), BF16), pltpu.VMEM((GROUP_ROWS, CB), F32),
                        pltpu.VMEM((PS + LRU_CONV_W - 1, SLOTS, CB), F32)]
        + [pltpu.VMEM(tile, F32) for _ in range(4)] + [pltpu.VMEM((4, CB, CB), BF16)],
        compiler_params=_cparams(("parallel", "arbitrary")),
        name="lru_seq_dec" if chunked else "lru_seq_ctx",
    )(*args)
    return (outs[0], None) if chunked else (outs[0], outs[1])


CONV_TC = 128


def _conv_seq_kernel(x_ref, mod_ref, wv_ref, wg_ref, bv_ref, bg_ref, w_ref, b_ref, o_ref,
                     h_ref, pad_ref, out_ref, *, chunked, mod_row):
    lo = (CONF_CONV_W - 1) // 2
    hi = CONF_CONV_W - 1 - lo
    _seq_modulated(x_ref, mod_ref, h_ref, mod_row)
    h = h_ref[...]
    val = _dot(h, wv_ref[...].astype(BF16)) + bv_ref[...]
    gate = _dot(h, wg_ref[...].astype(BF16)) + bg_ref[...]
    _fill_padded(pad_ref, val * _sigmoid(gate), lo, hi, chunked)

    def chunk(ci, carry):
        t0 = pl.multiple_of(ci * CONV_TC, CONV_TC)
        acc = b_ref[...] + w_ref[0:1, :] * pad_ref[pl.ds(t0, CONV_TC)]
        for k in range(1, CONF_CONV_W):
            acc = acc + w_ref[k:k + 1, :] * pad_ref[pl.ds(t0 + k, CONV_TC)]
        out_ref[pl.ds(t0, CONV_TC)] = acc
        return carry

    lax.fori_loop(0, PS // CONV_TC, chunk, 0)
    o_ref[...] = _from_time_major(out_ref[...])


def _conv_seq(x2, mods, layer, mod_row, p, j, chunked):
    rows = x2.shape[0]
    n_layers = p["conf_dw_b"].shape[0]
    w_pw1 = p["conf_w_pw1"]
    b_pw1 = p["conf_b_pw1"].reshape(n_layers, 1, 2 * D_MODEL)
    return pl.pallas_call(
        functools.partial(_conv_seq_kernel, chunked=chunked, mod_row=mod_row),
        grid=(rows // GROUP_ROWS, NB),
        in_specs=_seq_in_specs(layer) + [
            _chan_spec(D_MODEL, j), _chan_spec(D_MODEL, j, NB),
            _chan_spec(1, j), _chan_spec(1, j, NB),
            _chan_spec(CONF_CONV_W, j), _chan_spec(1, j)],
        out_specs=_group_spec(),
        out_shape=jax.ShapeDtypeStruct((rows, D_MODEL), F32),
        scratch_shapes=[pltpu.VMEM((GROUP_ROWS, D_MODEL), BF16),
                        pltpu.VMEM((PS + CONF_CONV_W - 1, SLOTS, CB), F32),
                        pltpu.VMEM((PS, SLOTS, CB), F32)],
        compiler_params=_cparams(("parallel", "arbitrary")),
        name="conv_seq_dec" if chunked else "conv_seq_ctx",
    )(x2, mods, w_pw1, w_pw1, b_pw1, b_pw1, p["conf_dw_w"],
      p["conf_dw_b"].reshape(n_layers, 1, D_MODEL))


def _attend(q, parts):
    lane = lax.broadcasted_iota(jnp.int32, (1, LANES), 1)
    m = q.shape[0]
    zero = jnp.zeros_like(q)
    qm = jnp.concatenate([jnp.where(lane < NA_HEAD_DIM, q, zero),
                          jnp.where(lane >= NA_HEAD_DIM, q, zero)], axis=0) * ATT_SCALE
    scores = []
    for k, _, bias in parts:
        s = _dot_t(qm, k)
        if bias is not None:
            ref, head = bias
            s = s + jnp.concatenate([ref[head], ref[head + 1]], axis=0)
        scores.append(s)
    mx = scores[0].max(axis=-1, keepdims=True)
    for s in scores[1:]:
        mx = jnp.maximum(mx, s.max(axis=-1, keepdims=True))
    den = 0.0
    acc = 0.0
    for s, (_, v, _) in zip(scores, parts):
        pr = jnp.exp(s - mx)
        den = den + pr.sum(axis=-1, keepdims=True)
        acc = acc + _dot(pr.astype(BF16), v)
    o = acc / den
    return jnp.where(lane < NA_HEAD_DIM, o[:m], o[m:])


def _attn_ctx_kernel(q_ref, k_ref, v_ref, o_ref):
    for s in range(D_MODEL // LANES):
        sl = slice(s * LANES, (s + 1) * LANES)
        k = k_ref[:, sl].astype(BF16)
        v = v_ref[:, sl].astype(BF16)
        o_ref[:, sl] = _attend(q_ref[:, sl], [(k, v, None)]).astype(o_ref.dtype)


def _attn_ctx(q, k, v, batch):
    blk = pl.BlockSpec((PS, D_MODEL), lambda b: (b, 0))
    return pl.pallas_call(
        _attn_ctx_kernel,
        grid=(batch,),
        in_specs=[blk, blk, blk],
        out_specs=blk,
        out_shape=jax.ShapeDtypeStruct(q.shape, BF16),
        compiler_params=_cparams(("parallel",)),
        name="attn_ctx",
    )(q, k, v)


def _attn_dec_kernel(q_ref, k_ref, v_ref, kc_ref, vc_ref, bias_ref, o_ref):
    i = pl.program_id(0)
    row0 = jnp.clip(Q_ROWS * i - NA_WIN_ROWS // 2, 0, GRID_H - WIN_ROWS_BLK)
    win = pl.ds(pl.multiple_of(row0 * GRID_W, GRID_W), WIN_KEYS)
    for s in range(DEC_SLABS):
        sl = slice(s * LANES, (s + 1) * LANES)
        kc = kc_ref[:, sl].astype(BF16)
        vc = vc_ref[:, sl].astype(BF16)
        o = _attend(q_ref[:, sl], [(k_ref[win, sl], v_ref[win, sl], (bias_ref, 2 * s)),
                                   (kc, vc, None)])
        o_ref[:, sl] = o.astype(o_ref.dtype)


def _qblk_window(i):
    row0 = min(max(Q_ROWS * i - NA_WIN_ROWS // 2, 0), GRID_H - WIN_ROWS_BLK)
    out = []
    for a in range(Q_ROWS):
        r = Q_ROWS * i + a
        rs = min(max(r - NA_WIN_ROWS // 2, 0), GRID_H - NA_WIN_ROWS)
        out.append((r, [rs <= row0 + w < rs + NA_WIN_ROWS for w in range(WIN_ROWS_BLK)]))
    return row0, out


BIAS_CLASSES = (0, 1, N_QBLK - 1)


def _bias_kernel(rpb_ref, o_ref):
    c = lax.broadcasted_iota(jnp.int32, (GRID_W, LANES), 0)
    l = lax.broadcasted_iota(jnp.int32, (GRID_W, LANES), 1)
    cs = jnp.clip(c - NA_WIN_COLS // 2, 0, GRID_W - NA_WIN_COLS)
    in_cols = (l >= cs) & (l < cs + NA_WIN_COLS)
    neg = jnp.full((GRID_W, LANES), NEG_BIG, F32)
    lo_half, hi_half = [], []
    for dr in range(2 * NA_WIN_ROWS - 1):
        row = jnp.broadcast_to(rpb_ref[dr:dr + 1, :], (GRID_W, LANES))
        t = pltpu.roll(row, LANES - (NA_WIN_COLS - 1), 1, stride=1, stride_axis=0)
        t = jnp.where(in_cols, t, NEG_BIG)
        lo_half.append(t)
        hi_half.append(pltpu.roll(t, GRID_W, 1))
    for cls, i in enumerate(BIAS_CLASSES):
        row0, qrows = _qblk_window(i)
        for a, (r, valid) in enumerate(qrows):
            for wp in range(WIN_ROWS_BLK // 2):
                halves = []
                for half, bank in enumerate((lo_half, hi_half)):
                    w = 2 * wp + half
                    halves.append(bank[row0 + w - r + NA_WIN_ROWS - 1] if valid[w] else neg)
                o_ref[cls, a * GRID_W:(a + 1) * GRID_W, wp * LANES:(wp + 1) * LANES] = (
                    jnp.where(l < GRID_W, halves[0], halves[1]))


def _attn_bias_table(rpb, j):
    nl, nh, ndr, ndc = rpb.shape
    ndr_pad = -(-ndr // SUBLANES) * SUBLANES
    rpb_p = jnp.pad(rpb, ((0, 0), (0, 0), (0, ndr_pad - ndr), (0, LANES - ndc)))
    return pl.pallas_call(
        _bias_kernel,
        grid=(nh,),
        in_specs=[pl.BlockSpec((None, None, ndr_pad, LANES), lambda h: (j, h, 0, 0))],
        out_specs=pl.BlockSpec((len(BIAS_CLASSES), None, Q_BLK, WIN_KEYS), lambda h: (0, h, 0, 0)),
        out_shape=jax.ShapeDtypeStruct((len(BIAS_CLASSES), nh, Q_BLK, WIN_KEYS), F32),
        compiler_params=_cparams(("parallel",)),
        name="attn_bias",
    )(rpb_p)


def _attn_dec(q, k, v, kc, vc, j, bias, batch):
    t = GRID_H * GRID_W
    assert DEC_SLABS * LANES == D_MODEL
    qblk = pl.BlockSpec((Q_BLK, D_MODEL), lambda i, b: (b * N_QBLK + i, 0))
    kvblk = pl.BlockSpec((None, t, D_MODEL), lambda i, b: (b, 0, 0))
    cblk = pl.BlockSpec((None, None, PS, D_MODEL), lambda i, b: (b, j, 0, 0))

    def bias_idx(i, b):
        cls = (i > 0).astype(jnp.int32) + (i == N_QBLK - 1).astype(jnp.int32)
        return (cls, 0, 0, 0)

    return pl.pallas_call(
        _attn_dec_kernel,
        grid=(N_QBLK, batch),
        in_specs=[qblk, kvblk, kvblk, cblk, cblk,
                  pl.BlockSpec((None, NA_HEADS, Q_BLK, WIN_KEYS), bias_idx,
                               pipeline_mode=pl.Buffered(1))],
        out_specs=qblk,
        out_shape=jax.ShapeDtypeStruct(q.shape, BF16),
        compiler_params=_cparams(("arbitrary", "arbitrary")),
        name="attn_dec",
    )(q, k.reshape(batch, t, D_MODEL), v.reshape(batch, t, D_MODEL), kc, vc, bias)


def _pffn_kernel(*refs, conf, final, mod_row, tm):
    refs = list(refs)
    x_ref, y_ref, mod_ref, wp_ref = refs[:4]
    pos = 4
    bp_ref = lng_ref = lnb_ref = fin_ref = None
    if conf:
        bp_ref, lng_ref, lnb_ref = refs[pos:pos + 3]
        pos += 3
    w1_ref, w2_ref = refs[pos:pos + 2]
    pos += 2
    if final:
        fin_ref = refs[pos]
        pos += 1
    o_ref, h2_ref, wpb_ref = refs[pos:pos + 3]
    f = pl.program_id(1)

    @pl.when(f == 0)
    def _():
        _, _, g1, sh2, sc2, _ = _mod_parts(mod_ref, mod_row)
        wpb_ref[...] = wp_ref[...].astype(BF16)
        for rb in range(tm // ROW_BLK):
            rows = slice(rb * ROW_BLK, (rb + 1) * ROW_BLK)
            if conf:
                z = y_ref[rows, :]
                mu = jnp.mean(z, axis=-1, keepdims=True)
                zc = z - mu
                var = jnp.mean(zc * zc, axis=-1, keepdims=True)
                zn = zc * lax.rsqrt(var + EPS) * lng_ref[...] + lnb_ref[...]
                y = (zn * _sigmoid(zn)).astype(BF16)
            else:
                y = y_ref[rows, :]
            proj = _dot(y, wpb_ref[...])
            if bp_ref is not None:
                proj = proj + bp_ref[...]
            x1 = x_ref[rows, :] + g1 * proj
            o_ref[rows, :] = x1
            h2_ref[rows, :] = (_rms(x1) * (1.0 + sc2) + sh2).astype(BF16)

    u = _dot(h2_ref[...], w1_ref[...])
    u = jnp.square(jnp.maximum(u, 0.0)).astype(BF16)
    g2 = _mod_parts(mod_ref, mod_row)[5]
    o_ref[...] += g2 * _dot(u, w2_ref[...])

    if final:
        @pl.when(f == pl.num_programs(1) - 1)
        def _():
            o_ref[...] = _rms(o_ref[...]) * fin_ref[...]


def _pffn(x2, y, mods, layer, make_mod_row, w_proj, j, w1, w2, b_proj=None, ln_g=None, ln_b=None,
          final_g=None):
    rows = x2.shape[0]
    conf = ln_g is not None
    final = final_g is not None
    tm, fk = (TM, FK_CONF) if conf else (TM_FULL, D_FF)
    wmode = pl.Buffered(1) if fk == D_FF else None
    mod_row = make_mod_row(tm)
    row_spec = pl.BlockSpec((tm, D_MODEL), lambda i, f: (i, 0))
    vec_spec = pl.BlockSpec((None, 1, D_MODEL), lambda i, f: (j, 0, 0))
    in_specs = [row_spec, row_spec,
                pl.BlockSpec((None, MOD_ROWS, 6 * D_MODEL), lambda i, f: (layer, 0, 0)),
                pl.BlockSpec((None, D_MODEL, D_MODEL), lambda i, f: (j, 0, 0),
                             pipeline_mode=pl.Buffered(1))]
    args = [x2, y, mods, w_proj]
    if conf:
        in_specs += [vec_spec, vec_spec, vec_spec]
        args += [v.reshape(v.shape[0], 1, D_MODEL) for v in (b_proj, ln_g, ln_b)]
    in_specs += [pl.BlockSpec((None, D_MODEL, fk), lambda i, f: (layer, 0, f), pipeline_mode=wmode),
                 pl.BlockSpec((None, fk, D_MODEL), lambda i, f: (layer, f, 0), pipeline_mode=wmode)]
    args += [w1, w2]
    if final:
        in_specs.append(pl.BlockSpec((1, D_MODEL), lambda i, f: (0, 0)))
        args.append(final_g.reshape(1, D_MODEL))
    return pl.pallas_call(
        functools.partial(_pffn_kernel, conf=conf, final=final, mod_row=mod_row, tm=tm),
        grid=(rows // tm, D_FF // fk),
        in_specs=in_specs,
        out_specs=row_spec,
        out_shape=jax.ShapeDtypeStruct((rows, D_MODEL), F32),
        scratch_shapes=[pltpu.VMEM((tm, D_MODEL), BF16), pltpu.VMEM((D_MODEL, D_MODEL), BF16)],
        compiler_params=_cparams(("parallel", "arbitrary")),
        name="pffn" + ("_conf" if conf else "") + ("_final" if final else ""),
    )(*args)


def _trunk(x, mods, is_ctx, p, state_lru, cache_k, cache_v, bias_tabs):
    bsz, t, d = x.shape
    rows = bsz * t
    assert d == D_MODEL and rows % GROUP_ROWS == 0 and rows % TM == 0
    if is_ctx:
        assert t == PS
    else:
        assert t == GROUP_ROWS == GRID_H * GRID_W and cache_k.shape[2] == PS
    x2 = x.reshape(rows, d)
    depth = mods.shape[0]
    row_mm = _make_mod_row(is_ctx, t, TM_MM)
    make_row_ffn = functools.partial(_make_mod_row, is_ctx, t)
    row_seq = _make_mod_row(is_ctx, t, GROUP_ROWS)
    states, ks, vs = [], [], []
    for i in range(depth):
        kind, j = i % 3, i // 3
        fin = p["final_g"] if i == depth - 1 else None
        ffn = functools.partial(_pffn, x2, mods=mods, layer=i, make_mod_row=make_row_ffn, j=j,
                                w1=p["w_ff1"], w2=p["w_ff2"], final_g=fin)
        if kind == 0:
            y, st = _lru_seq(x2, mods, i, row_seq, p, j, None if is_ctx else state_lru)
            if is_ctx:
                states.append(st)
            x2 = ffn(y=y, w_proj=p["lru_w_out"])
        elif kind == 1:
            zc = _conv_seq(x2, mods, i, row_seq, p, j, chunked=not is_ctx)
            x2 = ffn(y=zc, w_proj=p["conf_w_pw2"], b_proj=p["conf_b_pw2"],
                     ln_g=p["conf_ln_g"], ln_b=p["conf_ln_b"])
        else:
            if is_ctx:
                q, k, v, k_cache, v_cache = _qkv(x2, mods, i, row_mm, p["na_w_qkv"], j, True)
                o = _attn_ctx(q, k, v, bsz)
                ks.append(k_cache.reshape(bsz, t, NA_HEADS, NA_HEAD_DIM))
                vs.append(v_cache.reshape(bsz, t, NA_HEADS, NA_HEAD_DIM))
            else:
                q, k, v = _qkv(x2, mods, i, row_mm, p["na_w_qkv"], j, False)
                kc = cache_k.reshape(cache_k.shape[:3] + (d,))
                vc = cache_v.reshape(cache_v.shape[:3] + (d,))
                o = _attn_dec(q, k, v, kc, vc, j, bias_tabs[j], bsz)
            x2 = ffn(y=o, w_proj=p["na_w_o"])
    return x2.reshape(bsz, t, d), states, ks, vs


def kernel(x_prompt, x_sample, state_lru, cache_k, cache_v, c, c_ctx, w_mod, b_mod, w_ff1, w_ff2, lru_w_in, lru_conv_w, lru_conv_b, lru_w_a, lru_b_a, lru_w_x, lru_b_x, lru_lambda, lru_w_out, conf_w_pw1, conf_b_pw1, conf_dw_w, conf_dw_b, conf_ln_g, conf_ln_b, conf_w_pw2, conf_b_pw2, na_w_qkv, na_w_o, na_rpb, final_g):
    p = dict(w_ff1=_to_bf16(w_ff1), w_ff2=_to_bf16(w_ff2), lru_w_in=lru_w_in, lru_conv_w=lru_conv_w,
             lru_conv_b=lru_conv_b, lru_w_a=lru_w_a, lru_b_a=lru_b_a, lru_w_x=lru_w_x,
             lru_b_x=lru_b_x, lru_lambda=lru_lambda, lru_w_out=lru_w_out,
             conf_w_pw1=conf_w_pw1, conf_b_pw1=conf_b_pw1, conf_dw_w=conf_dw_w,
             conf_dw_b=conf_dw_b, conf_ln_g=conf_ln_g, conf_ln_b=conf_ln_b,
             conf_w_pw2=conf_w_pw2, conf_b_pw2=conf_b_pw2, na_w_qkv=na_w_qkv, na_w_o=na_w_o,
             final_g=final_g)
    dec_b = c.shape[0]
    assert 1 + dec_b <= MOD_ROWS
    cond8 = jnp.concatenate(
        [c_ctx[None, :], c, jnp.zeros((MOD_ROWS - 1 - dec_b, D_MODEL), F32)], axis=0)
    mods = _adaln(cond8, w_mod, b_mod)
    bias_tabs = [_attn_bias_table(na_rpb, j) for j in range(na_rpb.shape[0])]

    y_prompt, states, ks, vs = _trunk(x_prompt, mods, True, p, None, None, None, None)
    y_sample, _, _, _ = _trunk(x_sample, mods, False, p, state_lru, cache_k, cache_v, bias_tabs)

    new_state = jnp.stack([jnp.transpose(s, (1, 0, 2)) for s in states], axis=1)
    new_k = jnp.stack(ks, axis=1)
    new_v = jnp.stack(vs, axis=1)
    return (y_prompt, y_sample, new_state, new_k, new_v)
```
